```python
import math
import jax
import jax.numpy as jnp
from jax import lax
import numpy as np

D_MODEL = 1024
BATCH = 4
SEQ = 8192
DEPTH = 2
DEC_BATCH = 32
DEC_SEQ = 1
PAST_LEN = 16384
PAGE_SIZE = 128

N_MIXERS = 2
N_HEADS = 16
HEAD_DIM = D_MODEL // N_HEADS
IDX_HEADS = 8
IDX_DIM = 64
TOPK_MAX = 256
N_BUCKETS = 32
MAX_DISTANCE = 128
D_FF = 2816
N_EXPERTS = 8
TOP_K_EXPERTS = 2
D_FF_EXPERT = 3584
Q_BLOCK = 128
EPS = 1e-6
N_SB_LAYERS = (DEPTH + 1) // 2
N_DSA_LAYERS = DEPTH // 2
N_DENSE_FFN = (DEPTH + 1) // 2
N_MOE_FFN = DEPTH // 2
DSA_SPLITS = (D_MODEL, 2 * D_MODEL, 3 * D_MODEL, 3 * D_MODEL + IDX_HEADS * IDX_DIM, 3 * D_MODEL + IDX_HEADS * IDX_DIM + IDX_DIM)
DSA_IN = DSA_SPLITS[-1] + IDX_HEADS

kernel_name = "hybrid_stickbreak_dsa_decode_step"


def rmsnorm(x, g):
    xf = x.astype(jnp.float32)
    y = xf * lax.rsqrt(jnp.mean(xf * xf, axis=-1, keepdims=True) + EPS)
    return (y * g.astype(jnp.float32)).astype(x.dtype)


def ada_modulation(c, w_ada, b_ada):
    m = (jax.nn.silu(c) @ w_ada + b_ada)[:, None, :]
    return jnp.split(m, 6, axis=-1)


def sweep_query_blocks(fn, per_query, q_pos):
    t_q = q_pos.shape[0]
    blk = math.gcd(Q_BLOCK, t_q)
    nb = t_q // blk

    def split(a):
        return jnp.moveaxis(a.reshape(a.shape[0], nb, blk, *a.shape[2:]), 1, 0)

    out = lax.map(lambda args: fn(*args[0], args[1]), (tuple(split(a) for a in per_query), q_pos.reshape(nb, blk)))
    out = jnp.moveaxis(out, 0, 1)
    return out.reshape(out.shape[0], t_q, *out.shape[3:])


def stick_breaking_block(q, q_pos, segments, k_pos):
    logits = jnp.concatenate([jnp.einsum("bqhd,bkhd->bhqk", q, k) for k, _ in segments], axis=-1)
    logits = logits.astype(jnp.float32) * HEAD_DIM ** -0.5
    mask = k_pos[None, :] < q_pos[:, None]
    log_1mb = jnp.where(mask, jax.nn.log_sigmoid(-logits), 0.0)
    after = lax.cumsum(log_1mb, axis=3, reverse=True) - log_1mb
    w = jnp.where(mask, jnp.exp(jax.nn.log_sigmoid(logits) + after), 0.0)
    out, off = 0.0, 0
    for k, v in segments:
        n = k.shape[1]
        out = out + jnp.einsum("bhqk,bkhd->bqhd", w[..., off:off + n].astype(v.dtype), v)
        off += n
    return out


def stick_breaking_mixer(h, w_qkv, w_o, past):
    b, t, _ = h.shape
    q, k, v = (a.reshape(b, t, N_HEADS, HEAD_DIM) for a in jnp.split(h @ w_qkv, 3, axis=-1))
    if past is None:
        pos0 = 0
        segments = ((k, v),)
    else:
        cache_kv, page_table, layer = past
        pos0 = page_table.shape[1] * PAGE_SIZE
        kv_past = cache_kv[layer, page_table].reshape(b, pos0, 2, N_HEADS, HEAD_DIM)
        segments = ((kv_past[:, :, 0], kv_past[:, :, 1]), (k, v))
    q_pos = pos0 + jnp.arange(t, dtype=jnp.int32)
    k_pos = jnp.arange(pos0 + t, dtype=jnp.int32)
    o = sweep_query_blocks(lambda qb, pb: stick_breaking_block(qb, pb, segments, k_pos), (q,), q_pos)
    return o.reshape(b, t, D_MODEL) @ w_o, jnp.stack([k, v], axis=2)


def t5_bucket(rel):
    n = jnp.maximum(rel, 0)
    max_exact = N_BUCKETS // 2
    nf = jnp.maximum(n, 1).astype(jnp.float32)
    large = max_exact + (jnp.log(nf / max_exact) / math.log(MAX_DISTANCE / max_exact) * (N_BUCKETS - max_exact)).astype(jnp.int32)
    large = jnp.minimum(large, N_BUCKETS - 1)
    return jnp.where(n < max_exact, n, large)


def dsa_block(q, qi, wi, q_pos, kidx, k_pos, gather_kv, rel_bias, topk):
    s = jnp.einsum("bqhi,bki->bqhk", qi, kidx).astype(jnp.float32) * IDX_DIM ** -0.5
    score = jnp.einsum("bqh,bqhk->bqk", wi.astype(jnp.float32), jax.nn.relu(s))
    score = jnp.where(k_pos[None, :] <= q_pos[:, None], score, -jnp.inf)
    _, sel = lax.top_k(score, topk)
    sel_pos = k_pos[sel]
    valid = sel_pos <= q_pos[None, :, None]
    kv_sel = gather_kv(sel)
    k_sel, v_sel = kv_sel[:, :, :, 0], kv_sel[:, :, :, 1]
    logits = jnp.einsum("bqhd,bqkhd->bhqk", q, k_sel).astype(jnp.float32) * HEAD_DIM ** -0.5
    bias = rel_bias[t5_bucket(q_pos[None, :, None] - sel_pos)].astype(jnp.float32)
    logits = logits + jnp.transpose(bias, (0, 3, 1, 2))
    logits = jnp.where(valid[:, None], logits, -jnp.inf)
    p = jax.nn.softmax(logits, axis=-1)
    return jnp.einsum("bhqk,bqkhd->bqhd", p.astype(v_sel.dtype), v_sel)


def dsa_mixer(h, w_in, w_o, q_g, k_g, rel_bias, past):
    b, t, _ = h.shape
    q, k, v, qi, ki, wi = jnp.split(h @ w_in, DSA_SPLITS, axis=-1)
    q = rmsnorm(q.reshape(b, t, N_HEADS, HEAD_DIM), q_g)
    k = rmsnorm(k.reshape(b, t, N_HEADS, HEAD_DIM), k_g)
    v = v.reshape(b, t, N_HEADS, HEAD_DIM)
    qi = qi.reshape(b, t, IDX_HEADS, IDX_DIM)
    wi = wi * IDX_HEADS ** -0.5
    kv_new = jnp.stack([k, v], axis=2)
    if past is None:
        n_past = 0
        kidx_all = ki

        def gather_kv(sel):
            return jax.vmap(lambda a, i: a[i])(kv_new, sel)
    else:
        cache_kv, cache_kidx, page_table, layer = past
        n_past = page_table.shape[1] * PAGE_SIZE
        kidx_all = jnp.concatenate([cache_kidx[layer, page_table].reshape(b, n_past, IDX_DIM), ki], axis=1)

        def gather_kv(sel):
            sel_c = jnp.minimum(sel, n_past - 1)
            phys = jnp.take_along_axis(page_table, (sel_c // PAGE_SIZE).reshape(b, -1), axis=1).reshape(sel.shape)
            rows_past = cache_kv[layer, phys, sel_c % PAGE_SIZE]
            rows_new = jax.vmap(lambda a, i: a[i])(kv_new, jnp.clip(sel - n_past, 0, t - 1))
            return jnp.where((sel < n_past)[..., None, None, None], rows_past, rows_new)
    q_pos = n_past + jnp.arange(t, dtype=jnp.int32)
    k_pos = jnp.arange(n_past + t, dtype=jnp.int32)
    topk = min(TOPK_MAX, kidx_all.shape[1] // 4)
    o = sweep_query_blocks(
        lambda qb, qib, wib, pb: dsa_block(qb, qib, wib, pb, kidx_all, k_pos, gather_kv, rel_bias, topk),
        (q, qi, wi), q_pos)
    return o.reshape(b, t, D_MODEL) @ w_o, kv_new, ki


def swiglu(h, w_gate, w_up, w_down):
    return (jax.nn.silu(h @ w_gate) * (h @ w_up)) @ w_down


def moe_swiglu(h, w_router, w_gate, w_up, w_down):
    logits = (h @ w_router).astype(jnp.float32)
    top_v, top_i = lax.top_k(logits, TOP_K_EXPERTS)
    gates = jax.nn.softmax(top_v, axis=-1)
    combine = jnp.sum(jax.nn.one_hot(top_i, N_EXPERTS, dtype=jnp.float32) * gates[..., None], axis=-2)
    out = jnp.zeros_like(h)
    for e in range(N_EXPERTS):
        out = out + combine[..., e:e + 1].astype(h.dtype) * swiglu(h, w_gate[e], w_up[e], w_down[e])
    return out


def setup_inputs(seed: int = 0) -> dict:
    key = jax.random.key(seed)
    ks = iter(jax.random.split(key, 40))

    def nrm(shape, scale):
        return jax.random.normal(next(ks), shape, jnp.float32) * scale

    d = D_MODEL
    n_pages = PAST_LEN // PAGE_SIZE
    n_pool = (DEC_BATCH * n_pages * 5) // 4
    page_table = jax.random.permutation(next(ks), n_pool)[:DEC_BATCH * n_pages].reshape(DEC_BATCH, n_pages).astype(jnp.int32)
    return {
        "x_prompt": nrm((BATCH, SEQ, d), 1.0),
        "x_sample": nrm((DEC_BATCH, DEC_SEQ, d), 1.0),
        "c_prompt": nrm((BATCH, d), 1.0),
        "c_sample": nrm((DEC_BATCH, d), 1.0),
        "page_table": page_table,
        "cache_kv_sb": nrm((N_SB_LAYERS, n_pool, PAGE_SIZE, 2, N_HEADS, HEAD_DIM), 1.0),
        "cache_kv_dsa": nrm((N_DSA_LAYERS, n_pool, PAGE_SIZE, 2, N_HEADS, HEAD_DIM), 1.0),
        "cache_kidx_dsa": nrm((N_DSA_LAYERS, n_pool, PAGE_SIZE, IDX_DIM), 1.0),
        "norm_mix_g": 1.0 + nrm((DEPTH, d), 0.01),
        "norm_ffn_g": 1.0 + nrm((DEPTH, d), 0.01),
        "w_ada": nrm((DEPTH, d, 6 * d), 0.5 * d ** -0.5),
        "b_ada": nrm((DEPTH, 6 * d), 0.02),
        "sb_w_qkv": nrm((N_SB_LAYERS, d, 3 * d), d ** -0.5),
        "sb_w_o": nrm((N_SB_LAYERS, d, d), d ** -0.5),
        "dsa_w_in": nrm((N_DSA_LAYERS, d, DSA_IN), d ** -0.5),
        "dsa_w_o": nrm((N_DSA_LAYERS, d, d), d ** -0.5),
        "dsa_q_g": 1.0 + nrm((N_DSA_LAYERS, HEAD_DIM), 0.01),
        "dsa_k_g": 1.0 + nrm((N_DSA_LAYERS, HEAD_DIM), 0.01),
        "rel_bias": nrm((N_BUCKETS, N_HEADS), 0.5),
        "ffn_w_gate": nrm((N_DENSE_FFN, d, D_FF), d ** -0.5),
        "ffn_w_up": nrm((N_DENSE_FFN, d, D_FF), d ** -0.5),
        "ffn_w_down": nrm((N_DENSE_FFN, D_FF, d), D_FF ** -0.5),
        "moe_w_router": nrm((N_MOE_FFN, d, N_EXPERTS), d ** -0.5),
        "moe_w_gate": nrm((N_MOE_FFN, N_EXPERTS, d, D_FF_EXPERT), d ** -0.5),
        "moe_w_up": nrm((N_MOE_FFN, N_EXPERTS, d, D_FF_EXPERT), d ** -0.5),
        "moe_w_down": nrm((N_MOE_FFN, N_EXPERTS, D_FF_EXPERT, d), D_FF_EXPERT ** -0.5),
    }


def reference(x_prompt, x_sample, c_prompt, c_sample, page_table, cache_kv_sb, cache_kv_dsa, cache_kidx_dsa,
              norm_mix_g, norm_ffn_g, w_ada, b_ada, sb_w_qkv, sb_w_o, dsa_w_in, dsa_w_o, dsa_q_g, dsa_k_g, rel_bias,
              ffn_w_gate, ffn_w_up, ffn_w_down, moe_w_router, moe_w_gate, moe_w_up, moe_w_down):
    def trunk(x, c, sample):
        new_sb, new_dsa, new_kidx = [], [], []
        for i in range(DEPTH):
            sh_m, sc_m, g_m, sh_f, sc_f, g_f = ada_modulation(c, w_ada[i], b_ada[i])
            h = rmsnorm(x, norm_mix_g[i]) * (1.0 + sc_m) + sh_m
            j = i // N_MIXERS
            if i % N_MIXERS == 0:
                past = (cache_kv_sb, page_table, j) if sample else None
                y, kv = stick_breaking_mixer(h, sb_w_qkv[j], sb_w_o[j], past)
                new_sb.append(kv)
            else:
                past = (cache_kv_dsa, cache_kidx_dsa, page_table, j) if sample else None
                y, kv, ki = dsa_mixer(h, dsa_w_in[j], dsa_w_o[j], dsa_q_g[j], dsa_k_g[j], rel_bias, past)
                new_dsa.append(kv)
                new_kidx.append(ki)
            x = x + g_m * y
            h = rmsnorm(x, norm_ffn_g[i]) * (1.0 + sc_f) + sh_f
            f = i // 2
            if i % 2 == 0:
                y = swiglu(h, ffn_w_gate[f], ffn_w_up[f], ffn_w_down[f])
            else:
                y = moe_swiglu(h, moe_w_router[f], moe_w_gate[f], moe_w_up[f], moe_w_down[f])
            x = x + g_f * y
        return x, jnp.stack(new_sb), jnp.stack(new_dsa), jnp.stack(new_kidx)

    y_prompt, kv_sb_p, kv_dsa_p, kidx_p = trunk(x_prompt, c_prompt, False)
    y_sample, kv_sb_s, kv_dsa_s, kidx_s = trunk(x_sample, c_sample, True)
    return (y_prompt, y_sample, kv_sb_p, kv_dsa_p, kidx_p, kv_sb_s, kv_dsa_s, kidx_s)
```

```python
import functools
import math

import jax
import jax.numpy as jnp
import numpy as np
from jax import lax
from jax.experimental import pallas as pl
from jax.experimental.pallas import tpu as pltpu

F32 = jnp.float32
BF16 = jnp.bfloat16
I32 = jnp.int32
HIGHEST = lax.Precision.HIGHEST

EPS = 1e-6
TOPK_MAX = 256
MAX_DISTANCE = 128
LANES = 128
MASKED = -1e30
DEAD_LOG_WEIGHT = -110.0
INT_MIN = -(2 ** 31)
VMEM_LIMIT = 56 * 1024 * 1024


def _cparams(n_axes, vmem=VMEM_LIMIT):
    return pltpu.CompilerParams(dimension_semantics=("arbitrary",) * n_axes, vmem_limit_bytes=vmem)


def _fit_tile(tile, n, col0=0):
    tile = min(tile, n)
    while n % tile or col0 % tile:
        tile -= LANES
    return tile


def _dot_t(a, b):
    return lax.dot_general(a, b, (((1,), (1,)), ((), ())), preferred_element_type=F32)


def _softplus(s):
    return jnp.maximum(s, 0.0) + jnp.log(1.0 + jnp.exp(-jnp.abs(s)))


def _ada_kernel(c_ref, w_ref, b_ref, o_ref):
    c = c_ref[...]
    a = c * jax.nn.sigmoid(c)
    o_ref[0] = jnp.dot(a.astype(BF16), w_ref[0].astype(BF16), preferred_element_type=F32) + b_ref[0]


def ada_modulation(c, w_ada, b_ada):
    depth, d, n = w_ada.shape
    tn = d
    r = c.shape[0]
    return pl.pallas_call(
        _ada_kernel,
        grid=(depth, n // tn),
        in_specs=[pl.BlockSpec((r, d), lambda l, j: (0, 0)),
                  pl.BlockSpec((1, d, tn), lambda l, j: (l, 0, j)),
                  pl.BlockSpec((1, 1, tn), lambda l, j: (l, 0, j))],
        out_specs=pl.BlockSpec((1, r, tn), lambda l, j: (l, 0, j)),
        out_shape=jax.ShapeDtypeStruct((depth, r, n), F32),
        compiler_params=_cparams(2),
        name="ada_modulation",
    )(c, w_ada, b_ada.reshape(depth, 1, n))


def _norm_mod_kernel(x_ref, g_ref, sc_ref, sh_ref, *rest, router):
    x = x_ref[...]
    y = x * lax.rsqrt(jnp.mean(x * x, axis=-1, keepdims=True) + EPS) * g_ref[...]
    h = y * (1.0 + sc_ref[0]) + sh_ref[0]
    if not router:
        (h_ref,) = rest
        h_ref[...] = h.astype(h_ref.dtype)
        return
    wr_ref, h_ref, comb_ref = rest
    h_ref[...] = h.astype(h_ref.dtype)
    logits = jnp.dot(h.astype(BF16), wr_ref[...].astype(BF16), preferred_element_type=F32)
    n_e = logits.shape[1]
    lane = lax.broadcasted_iota(I32, logits.shape, 1)
    m1 = jnp.max(logits, axis=1, keepdims=True)
    i1 = jnp.min(jnp.where(logits == m1, lane, n_e), axis=1, keepdims=True)
    rest_logits = jnp.where(lane == i1, -jnp.inf, logits)
    m2 = jnp.max(rest_logits, axis=1, keepdims=True)
    i2 = jnp.min(jnp.where(rest_logits == m2, lane, n_e), axis=1, keepdims=True)
    e2 = jnp.exp(m2 - m1)
    g1 = 1.0 / (1.0 + e2)
    comb_ref[...] = jnp.where(lane == i1, g1, 0.0) + jnp.where(lane == i2, e2 * g1, 0.0)


def norm_mod(x, g, sc, sh, rows_per_mod, w_router=None, tm=512):
    m, d = x.shape
    tm = min(tm, m)
    r = sc.shape[1]
    mod_spec = pl.BlockSpec((1, r, d), lambda i: ((i * tm) // rows_per_mod, 0, 0))
    in_specs = [pl.BlockSpec((tm, d), lambda i: (i, 0)), pl.BlockSpec((1, d), lambda i: (0, 0)), mod_spec, mod_spec]
    args = [x, g.reshape(1, d), sc, sh]
    out_specs = pl.BlockSpec((tm, d), lambda i: (i, 0))
    out_shape = jax.ShapeDtypeStruct((m, d), BF16)
    router = w_router is not None
    if router:
        n_e = w_router.shape[1]
        in_specs.append(pl.BlockSpec((d, n_e), lambda i: (0, 0)))
        args.append(w_router)
        out_specs = (out_specs, pl.BlockSpec((tm, n_e), lambda i: (i, 0)))
        out_shape = (out_shape, jax.ShapeDtypeStruct((m, n_e), F32))
    return pl.pallas_call(
        functools.partial(_norm_mod_kernel, router=router),
        grid=(m // tm,), in_specs=in_specs, out_specs=out_specs, out_shape=out_shape,
        compiler_params=_cparams(1), name="norm_mod_router" if router else "norm_mod",
    )(*args)


def _mm_kernel(x_ref, w_ref, o_ref, wbf_ref, *, scale):
    @pl.when(pl.program_id(1) == 0)
    def _():
        wbf_ref[...] = w_ref[0].astype(BF16)
    acc = jnp.dot(x_ref[...].astype(BF16), wbf_ref[...], preferred_element_type=F32)
    if scale != 1.0:
        acc = acc * scale
    o_ref[...] = acc.astype(o_ref.dtype)


def matmul(x, w3, e, col0, n_cols, *, out_dtype=F32, scale=1.0, tm=1024, tn=1024):
    m, k = x.shape
    tm, tn = min(tm, m), _fit_tile(tn, n_cols, col0)
    assert m % tm == 0
    off = col0 // tn
    return pl.pallas_call(
        functools.partial(_mm_kernel, scale=scale),
        grid=(n_cols // tn, m // tm),
        in_specs=[pl.BlockSpec((tm, k), lambda j, i: (i, 0)),
                  pl.BlockSpec((1, k, tn), lambda j, i: (e, 0, j + off))],
        out_specs=pl.BlockSpec((tm, tn), lambda j, i: (i, j)),
        out_shape=jax.ShapeDtypeStruct((m, n_cols), out_dtype),
        scratch_shapes=[pltpu.VMEM((k, tn), BF16)],
        compiler_params=_cparams(2), name="matmul",
    )(x, w3)


def _mm_swiglu_kernel(x_ref, wg_ref, wu_ref, o_ref, wgbf_ref, wubf_ref):
    @pl.when(pl.program_id(1) == 0)
    def _():
        wgbf_ref[...] = wg_ref[0].astype(BF16)
        wubf_ref[...] = wu_ref[0].astype(BF16)
    x = x_ref[...]
    a = jnp.dot(x, wgbf_ref[...], preferred_element_type=F32)
    b = jnp.dot(x, wubf_ref[...], preferred_element_type=F32)
    o_ref[...] = (a * jax.nn.sigmoid(a) * b).astype(o_ref.dtype)


def matmul_swiglu(x, wg3, wu3, e, *, tm=1024, tn=512):
    m, k = x.shape
    n = wg3.shape[2]
    tm, tn = min(tm, m), _fit_tile(tn, n)
    assert m % tm == 0
    wspec = pl.BlockSpec((1, k, tn), lambda j, i: (e, 0, j))
    return pl.pallas_call(
        _mm_swiglu_kernel,
        grid=(n // tn, m // tm),
        in_specs=[pl.BlockSpec((tm, k), lambda j, i: (i, 0)), wspec, wspec],
        out_specs=pl.BlockSpec((tm, tn), lambda j, i: (i, j)),
        out_shape=jax.ShapeDtypeStruct((m, n), BF16),
        scratch_shapes=[pltpu.VMEM((k, tn), BF16), pltpu.VMEM((k, tn), BF16)],
        compiler_params=_cparams(2), name="matmul_swiglu",
    )(x, wg3, wu3)


def _mm_resid_kernel(x_ref, w_ref, res_ref, gate_ref, *rest, col):
    if col is None:
        o_ref, wbf_ref = rest
    else:
        rs_ref, o_ref, wbf_ref = rest

    @pl.when(pl.program_id(1) == 0)
    def _():
        wbf_ref[...] = w_ref[0].astype(BF16)
    y = jnp.dot(x_ref[...], wbf_ref[...], preferred_element_type=F32)
    if col is not None:
        y = y * rs_ref[:, col:col + 1]
    o_ref[...] = res_ref[...] + gate_ref[0] * y


def matmul_resid(x, w3, e, res, gate, rows_per_gate, row_scale=None, col=None, *, tm=512, tn=512):
    m, k = x.shape
    n = w3.shape[2]
    tm, tn = min(tm, m), _fit_tile(tn, n)
    assert m % tm == 0
    r = gate.shape[1]
    in_specs = [pl.BlockSpec((tm, k), lambda j, i: (i, 0)),
                pl.BlockSpec((1, k, tn), lambda j, i: (e, 0, j)),
                pl.BlockSpec((tm, tn), lambda j, i: (i, j)),
                pl.BlockSpec((1, r, tn), lambda j, i: ((i * tm) // rows_per_gate, 0, j))]
    args = [x, w3, res, gate]
    if col is not None:
        in_specs.append(pl.BlockSpec((tm, row_scale.shape[1]), lambda j, i: (i, 0)))
        args.append(row_scale)
    return pl.pallas_call(
        functools.partial(_mm_resid_kernel, col=col),
        grid=(n // tn, m // tm),
        in_specs=in_specs,
        out_specs=pl.BlockSpec((tm, tn), lambda j, i: (i, j)),
        out_shape=jax.ShapeDtypeStruct((m, n), F32),
        scratch_shapes=[pltpu.VMEM((k, tn), BF16)],
        compiler_params=_cparams(2), name="matmul_resid",
    )(*args)


def _mm_t_kernel(x_ref, wt_ref, g_ref, o_ref, wbf_ref, *, norm_tiles, head_dim):
    @pl.when(jnp.logical_and(pl.program_id(1) == 0, pl.program_id(2) == 0))
    def _():
        wbf_ref[...] = wt_ref[...].astype(BF16)
    acc = _dot_t(wbf_ref[...], x_ref[0])
    tn, tm = acc.shape

    @pl.when(pl.program_id(0) < norm_tiles)
    def _():
        a3 = acc.reshape(tn // head_dim, head_dim, tm)
        inv = lax.rsqrt(jnp.mean(a3 * a3, axis=1, keepdims=True) + EPS)
        o_ref[0] = (a3 * inv * g_ref[...].reshape(tn // head_dim, head_dim, 1)).reshape(tn, tm)

    @pl.when(pl.program_id(0) >= norm_tiles)
    def _():
        o_ref[0] = acc


def matmul_t(x, wt, b, t, *, norm_rows=0, gain=None, head_dim=64, tm=1024, tn=512):
    m, k = x.shape
    n = wt.shape[0]
    tm, tn = min(tm, t), math.gcd(min(tn, n), norm_rows) if norm_rows else min(tn, n)
    assert t % tm == 0 and n % tn == 0 and norm_rows % tn == 0 and tn % head_dim == 0
    if gain is None:
        gain = jnp.ones((n, 1), F32)
    return pl.pallas_call(
        functools.partial(_mm_t_kernel, norm_tiles=norm_rows // tn, head_dim=head_dim),
        grid=(n // tn, b, t // tm),
        in_specs=[pl.BlockSpec((1, tm, k), lambda j, bi, i: (bi, i, 0)),
                  pl.BlockSpec((tn, k), lambda j, bi, i: (j, 0)),
                  pl.BlockSpec((tn, 1), lambda j, bi, i: (j, 0))],
        out_specs=pl.BlockSpec((1, tn, tm), lambda j, bi, i: (bi, j, i)),
        out_shape=jax.ShapeDtypeStruct((b, n, t), F32),
        scratch_shapes=[pltpu.VMEM((tn, k), BF16)],
        compiler_params=_cparams(3), name="matmul_t",
    )(x.reshape(b, t, k), wt, gain)


def _head_scale(x, seg_ref, exp_ref, head_dim):
    ss = jnp.dot(x * x, seg_ref[...], preferred_element_type=F32, precision=HIGHEST)
    inv = lax.rsqrt(ss * (1.0 / head_dim) + EPS)
    return jnp.dot(inv, exp_ref[...], preferred_element_type=F32, precision=HIGHEST)


def _headnorm_q_kernel(x_ref, g_ref, seg_ref, exp_ref, o_ref, *, head_dim, scale):
    x = x_ref[...]
    y = x * _head_scale(x, seg_ref, exp_ref, head_dim) * g_ref[...]
    o_ref[...] = (y * scale).astype(o_ref.dtype)


def _headnorm_kv_kernel(x_ref, g_ref, seg_ref, exp_ref, o_ref, *, head_dim):
    d = g_ref.shape[1]
    k = x_ref[:, :d]
    o_ref[:, :d] = k * _head_scale(k, seg_ref, exp_ref, head_dim) * g_ref[...]
    o_ref[:, d:] = x_ref[:, d:]


def _head_mats(d, head_dim):
    head_of = np.arange(d) // head_dim
    seg = (head_of[:, None] == np.arange(d // head_dim)[None, :]).astype(np.float32)
    return jnp.asarray(seg), jnp.asarray(seg.T)


def headnorm_q(x, g, head_dim, scale, tm=512):
    m, d = x.shape
    tm = min(tm, m)
    seg, expand = _head_mats(d, head_dim)
    nh = d // head_dim
    return pl.pallas_call(
        functools.partial(_headnorm_q_kernel, head_dim=head_dim, scale=scale),
        grid=(m // tm,),
        in_specs=[pl.BlockSpec((tm, d), lambda i: (i, 0)), pl.BlockSpec((1, d), lambda i: (0, 0)),
                  pl.BlockSpec((d, nh), lambda i: (0, 0)), pl.BlockSpec((nh, d), lambda i: (0, 0))],
        out_specs=pl.BlockSpec((tm, d), lambda i: (i, 0)),
        out_shape=jax.ShapeDtypeStruct((m, d), BF16),
        compiler_params=_cparams(1), name="headnorm_q",
    )(x, jnp.tile(g, nh).reshape(1, d), seg, expand)


def headnorm_kv(kv, g, head_dim, tm=512):
    m, d2 = kv.shape
    d = d2 // 2
    tm = min(tm, m)
    seg, expand = _head_mats(d, head_dim)
    nh = d // head_dim
    return pl.pallas_call(
        functools.partial(_headnorm_kv_kernel, head_dim=head_dim),
        grid=(m // tm,),
        in_specs=[pl.BlockSpec((tm, d2), lambda i: (i, 0)), pl.BlockSpec((1, d), lambda i: (0, 0)),
                  pl.BlockSpec((d, nh), lambda i: (0, 0)), pl.BlockSpec((nh, d), lambda i: (0, 0))],
        out_specs=pl.BlockSpec((tm, d2), lambda i: (i, 0)),
        out_shape=jax.ShapeDtypeStruct((m, d2), F32),
        compiler_params=_cparams(1), name="headnorm_kv",
    )(kv, jnp.tile(g, nh).reshape(1, d), seg, expand)


def _suffix_matrix(n):
    r = lax.broadcasted_iota(I32, (n, n), 0)
    c = lax.broadcasted_iota(I32, (n, n), 1)
    return jnp.where(r > c, 1.0, 0.0).astype(BF16)


def _suffix_sum(l, u):
    hi = l.astype(BF16)
    lo = (l - hi.astype(F32)).astype(BF16)
    return jnp.dot(hi, u, preferred_element_type=F32) + jnp.dot(lo, u, preferred_element_type=F32)


def _sb_prompt_kernel(q_ref, k_ref, v_ref, o_ref, *, tq):
    tk = tq
    qi = pl.program_id(2)
    q2 = q_ref[0]
    lane = lax.broadcasted_iota(I32, (tq, LANES), 1)
    first = lane < LANES // 2
    qh = (jnp.where(first, q2, 0).astype(BF16), jnp.where(first, 0, q2).astype(BF16))
    u = _suffix_matrix(tk)
    row = lax.broadcasted_iota(I32, (tq, tk), 0)
    col = lax.broadcasted_iota(I32, (tq, tk), 1)
    strictly_before = col < row

    def block(j, carries, diagonal):
        start = pl.multiple_of(j * tk, tk)
        k2 = k_ref[0, :, pl.ds(start, tk)].astype(BF16)
        v2 = v_ref[0, :, pl.ds(start, tk)].astype(BF16)
        pvs, new_carries = [], []
        for hh in range(2):
            s = jnp.dot(qh[hh], k2, preferred_element_type=F32)
            sp = _softplus(s)
            l = -sp
            if diagonal:
                l = jnp.where(strictly_before, l, 0.0)
            after = _suffix_sum(l, u)
            logw = (s - sp) + after + carries[hh]
            w = jnp.exp(logw)
            if diagonal:
                w = jnp.where(strictly_before, w, 0.0)
            pvs.append(_dot_t(w.astype(BF16), v2))
            new_carries.append(carries[hh] + after[:, :1] + l[:, :1])
        return jnp.where(first, pvs[0], pvs[1]), new_carries

    zero = jnp.zeros((tq, 1), F32)
    acc, carries = block(qi, [zero, zero], True)

    def alive_of(cs):
        return (jnp.max(jnp.maximum(cs[0], cs[1])) > DEAD_LOG_WEIGHT).astype(I32)

    def cond(st):
        return jnp.logical_and(st[0] >= 0, st[1] > 0)

    def body(st):
        j, _, acc, c0, c1 = st
        pv, cs = block(j, [c0, c1], False)
        return j - 1, alive_of(cs), acc + pv, cs[0], cs[1]

    st = lax.while_loop(cond, body, (qi - 1, alive_of(carries), acc, carries[0], carries[1]))
    o_ref[0] = st[2].astype(o_ref.dtype)


def sb_attention_prompt(q, kv_t, tq):
    b, d2, t = kv_t.shape
    d = d2 // 2
    ng = d // LANES
    out = pl.pallas_call(
        functools.partial(_sb_prompt_kernel, tq=tq),
        grid=(b, ng, t // tq),
        in_specs=[pl.BlockSpec((1, tq, LANES), lambda bi, g, i: (bi, i, g)),
                  pl.BlockSpec((1, LANES, t), lambda bi, g, i: (bi, g, 0)),
                  pl.BlockSpec((1, LANES, t), lambda bi, g, i: (bi, ng + g, 0))],
        out_specs=pl.BlockSpec((1, tq, LANES), lambda bi, g, i: (bi, i, g)),
        out_shape=jax.ShapeDtypeStruct((b, t, d), BF16),
        compiler_params=_cparams(3), name="sb_attention_prompt",
    )(q.reshape(b, t, d), kv_t, kv_t)
    return out.reshape(b * t, d)


def _block_diag_q(q_row, n_heads, head_dim):
    d = q_row.shape[1]
    head_of_lane = lax.broadcasted_iota(I32, (n_heads, d), 1) // head_dim
    mine = head_of_lane == lax.broadcasted_iota(I32, (n_heads, d), 0)
    return jnp.where(mine, jnp.broadcast_to(q_row, (n_heads, d)), 0.0), mine


def _sb_sample_kernel(pt_ref, q_ref, cache_ref, o_ref, buf_ref, sem_ref, *, n_heads, head_dim):
    b = pl.program_id(0)
    n_pages = pt_ref.shape[1]
    page = buf_ref.shape[2]
    d = n_heads * head_dim

    def page_copy(p, slot):
        return pltpu.make_async_copy(cache_ref.at[pt_ref[b, p]], buf_ref.at[slot], sem_ref.at[slot])

    page_copy(n_pages - 1, 0).start()
    qbd, mine = _block_diag_q(q_ref[0], n_heads, head_dim)
    qbd = qbd.astype(BF16)
    u = _suffix_matrix(page)

    def cond(st):
        return jnp.logical_and(st[0] >= 0, st[1] > 0)

    def body(st):
        p, _, carry, acc = st
        slot = (n_pages - 1 - p) % 2
        page_copy(p, slot).wait()

        @pl.when(p >= 1)
        def _():
            page_copy(p - 1, 1 - slot).start()

        kp = buf_ref[slot, :d, :].astype(BF16)
        vp = buf_ref[slot, d:, :].astype(BF16)
        s = jnp.dot(qbd, kp, preferred_element_type=F32)
        sp = _softplus(s)
        l = -sp
        after = _suffix_sum(l, u)
        w = jnp.exp((s - sp) + after + carry)
        acc = acc + _dot_t(w.astype(BF16), vp)
        carry = carry + after[:, :1] + l[:, :1]
        alive = (jnp.max(carry) > DEAD_LOG_WEIGHT).astype(I32)
        return p - 1, alive, carry, acc

    st = lax.while_loop(cond, body, (jnp.int32(n_pages - 1), jnp.int32(1),
                                     jnp.zeros((n_heads, 1), F32), jnp.zeros((n_heads, d), F32)))
    p_end = st[0]

    @pl.when(p_end >= 0)
    def _():
        page_copy(p_end, (n_pages - 1 - p_end) % 2).wait()

    o_ref[0] = jnp.sum(jnp.where(mine, st[3], 0.0), axis=0, keepdims=True).astype(o_ref.dtype)


def sb_attention_sample(q, cache, page_table, n_heads, head_dim):
    b, d = q.shape
    page = cache.shape[2]
    out = pl.pallas_call(
        functools.partial(_sb_sample_kernel, n_heads=n_heads, head_dim=head_dim),
        grid_spec=pltpu.PrefetchScalarGridSpec(
            num_scalar_prefetch=1, grid=(b,),
            in_specs=[pl.BlockSpec((1, 1, d), lambda i, pt: (i, 0, 0)),
                      pl.BlockSpec(memory_space=pl.ANY)],
            out_specs=pl.BlockSpec((1, 1, d), lambda i, pt: (i, 0, 0)),
            scratch_shapes=[pltpu.VMEM((2, 2 * d, page), F32), pltpu.SemaphoreType.DMA((2,))]),
        out_shape=jax.ShapeDtypeStruct((b, 1, d), BF16),
        compiler_params=_cparams(1), name="sb_attention_sample",
    )(page_table, q.reshape(b, 1, d), cache)
    return out.reshape(b, d)


def _ordered_key(score):
    bits = lax.bitcast_convert_type(score, I32)
    key = jnp.where(bits < 0, bits ^ 0x7FFFFFFF, bits)
    return jnp.where(score == 0.0, 0, key)


def _head_weights(w_idx, idx_heads, idx_dim):
    return (w_idx * idx_heads ** -0.5).astype(BF16).astype(F32) * idx_dim ** -0.5


def _relu_bf16(s):
    return jnp.maximum(s, 0.0).astype(BF16).astype(F32)


def _bisect_threshold(count_ge, topk, rows):
    lo = jnp.where(count_ge(jnp.zeros((rows, 1), I32)) >= topk, 0, INT_MIN).astype(I32)

    def step(it, lo):
        cand = lo | lax.shift_left(jnp.int32(1), 30 - it)
        return jnp.where(count_ge(cand) >= topk, cand, lo)

    return lax.fori_loop(0, 31, step, lo)


def _tie_cutoff(count_tie_before, need, n_bits, rows):
    def step(it, j):
        cand = j | lax.shift_left(jnp.int32(1), n_bits - 1 - it)
        return jnp.where(count_tie_before(cand) < need, cand, j)

    return lax.fori_loop(0, n_bits, step, jnp.zeros((rows, 1), I32))


def _selected(key, thr, col, cutoff):
    return jnp.where(key > thr, 1, jnp.where(key == thr, jnp.where(col <= cutoff, 1, 0), 0))


def _dsa_select_kernel(qi_ref, qw_ref, kt_ref, o_ref, key_ref, *, tq, idx_heads, idx_dim, topk):
    tk = tq
    i = pl.program_id(1)
    n_blocks = o_ref.shape[2] // tk
    qi = qi_ref[0].astype(BF16)
    wi = _head_weights(qw_ref[0][:, idx_dim:idx_dim + idx_heads], idx_heads, idx_dim)
    row = lax.broadcasted_iota(I32, (tq, tk), 0)
    col = lax.broadcasted_iota(I32, (tq, tk), 1)
    zeros = jnp.zeros((idx_dim, tk), BF16)

    def score_block(j, _):
        start = pl.multiple_of(j * tk, tk)
        kt = kt_ref[0, :, pl.ds(start, tk)].astype(BF16)
        ka = jnp.concatenate([kt, zeros], axis=0)
        kb = jnp.concatenate([zeros, kt], axis=0)
        score = jnp.zeros((tq, tk), F32)
        for g in range(idx_heads // 2):
            q2 = qi[:, g * LANES:(g + 1) * LANES]
            score = score + wi[:, 2 * g:2 * g + 1] * _relu_bf16(jnp.dot(q2, ka, preferred_element_type=F32))
            score = score + wi[:, 2 * g + 1:2 * g + 2] * _relu_bf16(jnp.dot(q2, kb, preferred_element_type=F32))
        score = jnp.where(col + (j - i) * tk <= row, score, -jnp.inf)
        key_ref[:, pl.ds(start, tk)] = _ordered_key(score)
        return 0

    lax.fori_loop(0, i + 1, score_block, 0)

    def count_ge(cand):
        def body(j, acc):
            hit = jnp.where(key_ref[:, pl.ds(pl.multiple_of(j * tk, tk), tk)] >= cand, 1, 0)
            for c in range(tk // LANES):
                acc = acc + hit[:, c * LANES:(c + 1) * LANES]
            return acc
        acc = lax.fori_loop(0, i + 1, body, jnp.zeros((tq, LANES), I32))
        return jnp.sum(acc, axis=1, keepdims=True)

    thr = _bisect_threshold(count_ge, topk, tq)
    need = topk - count_ge(thr + 1)
    t_keys = n_blocks * tk

    def count_tie_before(cand):
        def body(j, acc):
            kt = key_ref[:, pl.ds(pl.multiple_of(j * tk, tk), tk)]
            hit = jnp.where(kt == thr, jnp.where(col + j * tk < cand, 1, 0), 0)
            for c in range(tk // LANES):
                acc = acc + hit[:, c * LANES:(c + 1) * LANES]
            return acc
        acc = lax.fori_loop(0, i + 1, body, jnp.zeros((tq, LANES), I32))
        return jnp.sum(acc, axis=1, keepdims=True)

    cutoff = lax.cond(jnp.max(count_ge(thr)) > topk,
                      lambda: _tie_cutoff(count_tie_before, need, (t_keys - 1).bit_length(), tq),
                      lambda: jnp.full((tq, 1), t_keys, I32))

    def write_block(j, _):
        start = pl.multiple_of(j * tk, tk)
        keep = _selected(key_ref[:, pl.ds(start, tk)], thr, col + j * tk, cutoff)
        keep = jnp.where(col + (j - i) * tk <= row, keep, 0)
        o_ref[0, :, pl.ds(start, tk)] = jnp.where(keep > 0, 0.0, MASKED).astype(o_ref.dtype)
        return 0

    lax.fori_loop(0, i + 1, write_block, 0)

    def fill_block(j, _):
        o_ref[0, :, pl.ds(pl.multiple_of(j * tk, tk), tk)] = jnp.full((tq, tk), MASKED, o_ref.dtype)
        return 0

    lax.fori_loop(i + 1, n_blocks, fill_block, 0)


def dsa_select_prompt(idx, k_idx_t, tq, idx_heads, idx_dim, topk):
    b, _, t = k_idx_t.shape
    w = idx.shape[1]
    nq = idx_heads * idx_dim
    assert nq % LANES == 0 and w == nq + LANES and idx_dim * 2 == LANES
    idx3 = idx.reshape(b, t, w)
    kw_block = nq // LANES
    return pl.pallas_call(
        functools.partial(_dsa_select_kernel, tq=tq, idx_heads=idx_heads, idx_dim=idx_dim, topk=topk),
        grid=(b, t // tq),
        in_specs=[pl.BlockSpec((1, tq, nq), lambda bi, i: (bi, i, 0)),
                  pl.BlockSpec((1, tq, LANES), lambda bi, i: (bi, i, kw_block)),
                  pl.BlockSpec((1, idx_dim, t), lambda bi, i: (bi, 0, 0))],
        out_specs=pl.BlockSpec((1, tq, t), lambda bi, i: (bi, i, 0)),
        out_shape=jax.ShapeDtypeStruct((b, t, t), BF16),
        scratch_shapes=[pltpu.VMEM((tq, t), I32)],
        compiler_params=_cparams(2), name="dsa_select_prompt",
    )(idx3, idx3, k_idx_t)


def _t5_bucket(rel, n_buckets):
    n = jnp.maximum(rel, 0)
    max_exact = n_buckets // 2
    nf = jnp.maximum(n, 1).astype(F32)
    large = max_exact + (jnp.log(nf / max_exact) / math.log(MAX_DISTANCE / max_exact)
                         * (n_buckets - max_exact)).astype(I32)
    large = jnp.minimum(large, n_buckets - 1)
    return jnp.where(n < max_exact, n, large)


def _bias_tiles_kernel(relb_ref, o_ref, *, tq, n_buckets):
    h = pl.program_id(0)
    row = lax.broadcasted_iota(I32, (tq, tq), 0)
    col = lax.broadcasted_iota(I32, (tq, tq), 1)
    far = relb_ref[n_buckets - 1, h]
    for s in range(2):
        bucket = _t5_bucket(row - col + s * tq, n_buckets)
        val = jnp.zeros((tq, tq), F32)
        for bk in range(n_buckets):
            val = jnp.where(bucket == bk, relb_ref[bk, h] - far, val)
        o_ref[0, s] = val


def bias_tiles(rel_bias, tq):
    n_buckets, n_heads = rel_bias.shape
    assert tq >= MAX_DISTANCE
    return pl.pallas_call(
        functools.partial(_bias_tiles_kernel, tq=tq, n_buckets=n_buckets),
        grid=(n_heads,),
        in_specs=[pl.BlockSpec(memory_space=pltpu.SMEM)],
        out_specs=pl.BlockSpec((1, 2, tq, tq), lambda h: (h, 0, 0, 0)),
        out_shape=jax.ShapeDtypeStruct((n_heads, 2, tq, tq), F32),
        compiler_params=_cparams(1), name="bias_tiles",
    )(rel_bias)


def _dsa_attn_kernel(qt_ref, kt_ref, q_ref, kv_ref, mb_ref, bias_ref, o_ref, m_ref, l_ref, acc_ref,
                     *, tq, n_heads):
    p = pl.program_id(1)
    qi, kj = qt_ref[p], kt_ref[p]
    d = q_ref.shape[2]
    lane = lax.broadcasted_iota(I32, (tq, LANES), 1)
    first = lane < LANES // 2

    @pl.when(kj == 0)
    def _():
        m_ref[...] = jnp.full(m_ref.shape, MASKED, F32)
        l_ref[...] = jnp.zeros(l_ref.shape, F32)
        acc_ref[...] = jnp.zeros(acc_ref.shape, F32)

    def run(with_bias):
        mb = mb_ref[0].astype(F32)
        for g in range(n_heads // 2):
            cols = slice(g * LANES, (g + 1) * LANES)
            q2 = q_ref[0, :, cols]
            k2 = kv_ref[0, g * LANES:(g + 1) * LANES, :].astype(BF16)
            v2 = kv_ref[0, d + g * LANES:d + (g + 1) * LANES, :].astype(BF16)
            pvs, alphas = [], []
            for hh in range(2):
                h = 2 * g + hh
                qh = jnp.where(first, q2, 0) if hh == 0 else jnp.where(first, 0, q2)
                s = jnp.dot(qh.astype(BF16), k2, preferred_element_type=F32) + mb
                if with_bias:
                    s = s + bias_ref[h, qi - kj]
                m_old = m_ref[h]
                m_new = jnp.maximum(m_old, jnp.max(s, axis=1, keepdims=True))
                alpha = jnp.exp(m_old - m_new)
                pr = jnp.exp(s - m_new)
                l_ref[h] = alpha * l_ref[h] + jnp.sum(pr, axis=1, keepdims=True)
                m_ref[h] = m_new
                pvs.append(_dot_t(pr.astype(BF16), v2))
                alphas.append(alpha)
            acc_ref[:, cols] = (jnp.where(first, alphas[0], alphas[1]) * acc_ref[:, cols]
                                + jnp.where(first, pvs[0], pvs[1]))

    near = kj >= qi - 1

    @pl.when(near)
    def _():
        run(True)

    @pl.when(jnp.logical_not(near))
    def _():
        run(False)

    @pl.when(kj == qi)
    def _():
        for g in range(n_heads // 2):
            cols = slice(g * LANES, (g + 1) * LANES)
            inv = jnp.where(first, 1.0 / l_ref[2 * g], 1.0 / l_ref[2 * g + 1])
            o_ref[0, :, cols] = (acc_ref[:, cols] * inv).astype(o_ref.dtype)


def dsa_attention_prompt(q, kv_t, mask_bias, bias, tq, n_heads):
    b, _, t = kv_t.shape
    d = q.shape[1]
    nb = t // tq
    pairs = [(i, j) for i in range(nb) for j in range(i + 1)]
    q_tab = jnp.asarray(np.array([p[0] for p in pairs], np.int32))
    k_tab = jnp.asarray(np.array([p[1] for p in pairs], np.int32))
    out = pl.pallas_call(
        functools.partial(_dsa_attn_kernel, tq=tq, n_heads=n_heads),
        grid_spec=pltpu.PrefetchScalarGridSpec(
            num_scalar_prefetch=2, grid=(b, len(pairs)),
            in_specs=[pl.BlockSpec((1, tq, d), lambda bi, p, qt, kt: (bi, qt[p], 0)),
                      pl.BlockSpec((1, 2 * d, tq), lambda bi, p, qt, kt: (bi, 0, kt[p])),
                      pl.BlockSpec((1, tq, tq), lambda bi, p, qt, kt: (bi, qt[p], kt[p])),
                      pl.BlockSpec((n_heads, 2, tq, tq), lambda bi, p, qt, kt: (0, 0, 0, 0))],
            out_specs=pl.BlockSpec((1, tq, d), lambda bi, p, qt, kt: (bi, qt[p], 0)),
            scratch_shapes=[pltpu.VMEM((n_heads, tq, 1), F32), pltpu.VMEM((n_heads, tq, 1), F32),
                            pltpu.VMEM((tq, d), F32)]),
        out_shape=jax.ShapeDtypeStruct((b, t, d), BF16),
        compiler_params=_cparams(2), name="dsa_attention_prompt",
    )(q_tab, k_tab, q.reshape(b, t, d), kv_t, mask_bias, bias)
    return out.reshape(b * t, d)


PAGES_PER_STEP = 8


def _page_specs(page_shape, n_pages_per_step):
    def spec(u):
        return pl.BlockSpec((1,) + page_shape, lambda bi, s, pt: (pt[bi, s * n_pages_per_step + u], 0, 0))
    return [spec(u) for u in range(n_pages_per_step)]


def _dsa_sample_score_kernel(pt_ref, qi_ref, wi_ref, *rest, idx_heads, idx_dim):
    page_refs, o_ref = rest[:-1], rest[-1]
    qi = qi_ref[0].astype(BF16)
    w = _head_weights(wi_ref[0], idx_heads, idx_dim)
    page = page_refs[0].shape[2]
    for u_, ref in enumerate(page_refs):
        s = jnp.dot(qi, ref[0].astype(BF16), preferred_element_type=F32)
        o_ref[0, :, u_ * page:(u_ + 1) * page] = jnp.sum(w * _relu_bf16(s), axis=0, keepdims=True)


def dsa_sample_scores(qi, wi, cache_kidx, page_table, idx_heads, idx_dim):
    b, n_pages = page_table.shape
    page = cache_kidx.shape[2]
    u = math.gcd(PAGES_PER_STEP, n_pages)
    out = pl.pallas_call(
        functools.partial(_dsa_sample_score_kernel, idx_heads=idx_heads, idx_dim=idx_dim),
        grid_spec=pltpu.PrefetchScalarGridSpec(
            num_scalar_prefetch=1, grid=(b, n_pages // u),
            in_specs=[pl.BlockSpec((1, idx_heads, idx_dim), lambda bi, s, pt: (bi, 0, 0)),
                      pl.BlockSpec((1, idx_heads, 1), lambda bi, s, pt: (bi, 0, 0))]
                     + _page_specs((idx_dim, page), u),
            out_specs=pl.BlockSpec((1, 1, u * page), lambda bi, s, pt: (bi, 0, s))),
        out_shape=jax.ShapeDtypeStruct((b, 1, n_pages * page), F32),
        compiler_params=_cparams(2), name="dsa_sample_scores",
    )(page_table, qi.reshape(b, idx_heads, idx_dim), wi.reshape(b, idx_heads, 1), *([cache_kidx] * u))
    return out.reshape(b, n_pages * page)


def _dsa_sample_select_kernel(score_ref, idx_ref, rep_ref, seg_ref, mb_ref, newb_ref, key_ref,
                              *, idx_heads, idx_dim, topk):
    b, n_past = score_ref.shape
    nq = idx_heads * idx_dim
    qi = idx_ref[:, :nq].astype(BF16).astype(F32)
    ki = idx_ref[:, nq:nq + idx_dim].astype(BF16).astype(F32)
    wi = _head_weights(idx_ref[:, nq + idx_dim:nq + idx_dim + idx_heads], idx_heads, idx_dim)
    prod = qi * jnp.dot(ki, rep_ref[...], preferred_element_type=F32, precision=HIGHEST)
    s_new = jnp.dot(prod, seg_ref[...], preferred_element_type=F32, precision=HIGHEST)
    score_new = jnp.sum(wi * _relu_bf16(s_new), axis=1, keepdims=True)
    key_new = _ordered_key(score_new)
    key_ref[...] = _ordered_key(score_ref[...])
    chunk = 1024

    def count_ge(cand):
        def body(c, acc):
            hit = jnp.where(key_ref[:, pl.ds(pl.multiple_of(c * chunk, chunk), chunk)] >= cand, 1, 0)
            for cc in range(chunk // LANES):
                acc = acc + hit[:, cc * LANES:(cc + 1) * LANES]
            return acc
        acc = lax.fori_loop(0, n_past // chunk, body, jnp.zeros((b, LANES), I32))
        return jnp.sum(acc, axis=1, keepdims=True) + jnp.where(key_new >= cand, 1, 0)

    thr = _bisect_threshold(count_ge, topk, b)
    need = topk - count_ge(thr + 1)
    col = lax.broadcasted_iota(I32, (b, chunk), 1)

    def count_tie_before(cand):
        def body(c, acc):
            kt = key_ref[:, pl.ds(pl.multiple_of(c * chunk, chunk), chunk)]
            hit = jnp.where(kt == thr, jnp.where(col + c * chunk < cand, 1, 0), 0)
            for cc in range(chunk // LANES):
                acc = acc + hit[:, cc * LANES:(cc + 1) * LANES]
            return acc
        acc = lax.fori_loop(0, n_past // chunk, body, jnp.zeros((b, LANES), I32))
        return (jnp.sum(acc, axis=1, keepdims=True)
                + jnp.where(key_new == thr, jnp.where(n_past < cand, 1, 0), 0))

    cutoff = _tie_cutoff(count_tie_before, need, n_past.bit_length(), b)

    def write_chunk(c, _):
        start = pl.multiple_of(c * chunk, chunk)
        keep = _selected(key_ref[:, pl.ds(start, chunk)], thr, col + c * chunk, cutoff)
        mb_ref[:, pl.ds(start, chunk)] = jnp.where(keep > 0, 0.0, MASKED)
        return 0

    lax.fori_loop(0, n_past // chunk, write_chunk, 0)
    new_bias = jnp.where(_selected(key_new, thr, n_past, cutoff) > 0, 0.0, MASKED)
    newb_ref[...] = jnp.broadcast_to(new_bias, newb_ref.shape)


def dsa_sample_select(scores, idx, idx_heads, idx_dim, topk):
    b, n_past = scores.shape
    assert n_past % 1024 == 0
    nq = idx_heads * idx_dim
    seg = jnp.asarray((np.arange(nq)[:, None] // idx_dim == np.arange(idx_heads)[None, :]).astype(np.float32))
    rep = jnp.asarray((np.arange(idx_dim)[:, None] == np.arange(nq)[None, :] % idx_dim).astype(np.float32))
    return pl.pallas_call(
        functools.partial(_dsa_sample_select_kernel, idx_heads=idx_heads, idx_dim=idx_dim, topk=topk),
        out_shape=(jax.ShapeDtypeStruct((b, n_past), F32), jax.ShapeDtypeStruct((b, LANES), F32)),
        scratch_shapes=[pltpu.VMEM((b, n_past), I32)],
        compiler_params=pltpu.CompilerParams(vmem_limit_bytes=VMEM_LIMIT), name="dsa_sample_select",
    )(scores, idx, rep, seg)


def _dsa_sample_attn_kernel(pt_ref, q_ref, mb_ref, newb_ref, kvn_ref, relbt_ref, *rest,
                            n_heads, head_dim, n_buckets, n_pages):
    page_refs, (o_ref, m_ref, l_ref, acc_ref) = rest[:-4], rest[-4:]
    step = pl.program_id(1)
    u = len(page_refs)
    n_steps = n_pages // u
    page = page_refs[0].shape[2]
    d = n_heads * head_dim
    n_past = n_pages * page
    qbd, mine = _block_diag_q(q_ref[0].astype(F32), n_heads, head_dim)
    qbd = qbd.astype(BF16)
    relbt = relbt_ref[...]

    @pl.when(step == 0)
    def _():
        m_ref[...] = jnp.full(m_ref.shape, MASKED, F32)
        l_ref[...] = jnp.zeros(l_ref.shape, F32)
        acc_ref[...] = jnp.zeros(acc_ref.shape, F32)

    def update(s, values_fn):
        m_old = m_ref[...]
        m_new = jnp.maximum(m_old, jnp.max(s, axis=1, keepdims=True))
        alpha = jnp.exp(m_old - m_new)
        pr = jnp.exp(s - m_new)
        l_ref[...] = alpha * l_ref[...] + jnp.sum(pr, axis=1, keepdims=True)
        m_ref[...] = m_new
        acc_ref[...] = alpha * acc_ref[...] + values_fn(pr)

    bucket_iota = lax.broadcasted_iota(I32, (n_buckets, page), 0)
    for u_, ref in enumerate(page_refs):
        k_pos = (step * u + u_) * page + lax.broadcasted_iota(I32, (1, page), 1)
        onehot = jnp.where(bucket_iota == _t5_bucket(n_past - k_pos, n_buckets), 1.0, 0.0)
        bias = jnp.dot(relbt, onehot, preferred_element_type=F32, precision=HIGHEST)
        kp = ref[0, :d, :].astype(BF16)
        vp = ref[0, d:, :].astype(BF16)
        s = (jnp.dot(qbd, kp, preferred_element_type=F32) + bias
             + mb_ref[0, :, u_ * page:(u_ + 1) * page])
        update(s, lambda pr, vp=vp: _dot_t(pr.astype(BF16), vp))

    @pl.when(step == n_steps - 1)
    def _():
        k_new = kvn_ref[0, :, :d].astype(BF16).astype(F32)
        v_new = kvn_ref[0, :, d:]
        s_new = (jnp.sum(qbd.astype(F32) * k_new, axis=1, keepdims=True)
                 + relbt[:, 0:1] + newb_ref[0, :, 0:1])
        update(s_new, lambda pr: pr * v_new)
        out = acc_ref[...] * (1.0 / l_ref[...])
        o_ref[0] = jnp.sum(jnp.where(mine, out, 0.0), axis=0, keepdims=True).astype(o_ref.dtype)


def dsa_attention_sample(q, mask_bias, new_bias, kv_new, rel_bias, cache, page_table, n_heads, head_dim):
    b, d = q.shape
    n_pages = page_table.shape[1]
    page = cache.shape[2]
    u = math.gcd(PAGES_PER_STEP, n_pages)
    n_buckets = rel_bias.shape[0]
    out = pl.pallas_call(
        functools.partial(_dsa_sample_attn_kernel, n_heads=n_heads, head_dim=head_dim, n_buckets=n_buckets,
                          n_pages=n_pages),
        grid_spec=pltpu.PrefetchScalarGridSpec(
            num_scalar_prefetch=1, grid=(b, n_pages // u),
            in_specs=[pl.BlockSpec((1, 1, d), lambda bi, s, pt: (bi, 0, 0)),
                      pl.BlockSpec((1, 1, u * page), lambda bi, s, pt: (bi, 0, s)),
                      pl.BlockSpec((1, 1, LANES), lambda bi, s, pt: (bi, 0, 0)),
                      pl.BlockSpec((1, 1, 2 * d), lambda bi, s, pt: (bi, 0, 0)),
                      pl.BlockSpec((n_heads, n_buckets), lambda bi, s, pt: (0, 0))]
                     + _page_specs((2 * d, page), u),
            out_specs=pl.BlockSpec((1, 1, d), lambda bi, s, pt: (bi, 0, 0)),
            scratch_shapes=[pltpu.VMEM((n_heads, 1), F32), pltpu.VMEM((n_heads, 1), F32),
                            pltpu.VMEM((n_heads, d), F32)]),
        out_shape=jax.ShapeDtypeStruct((b, 1, d), BF16),
        compiler_params=_cparams(2), name="dsa_attention_sample",
    )(page_table, q.reshape(b, 1, d), mask_bias.reshape(b, 1, -1), new_bias.reshape(b, 1, LANES),
      kv_new.reshape(b, 1, 2 * d), rel_bias.T, *([cache] * u))
    return out.reshape(b, d)


def _trunk(x, mods, rows_per_mod, sample, page_table, cache_kv_sb, cache_kv_dsa, cache_kidx_dsa,
           norm_mix_g, norm_ffn_g, sb_w_qkv, sb_w_o, dsa_w_in, dsa_w_o, dsa_q_g, dsa_k_g, rel_bias,
           ffn_w_gate, ffn_w_up, ffn_w_down, moe_w_router, moe_w_gate, moe_w_up, moe_w_down, tq):
    b, t, d = x.shape
    m = b * t
    head_dim = dsa_q_g.shape[1]
    n_heads = d // head_dim
    idx_dim = cache_kidx_dsa.shape[-1]
    idx_heads = (dsa_w_in.shape[2] - 3 * d - idx_dim) // (idx_dim + 1)
    page = cache_kv_sb.shape[2]
    n_experts = moe_w_router.shape[2]
    q_scale = head_dim ** -0.5
    assert head_dim * 2 == LANES and q_scale == 0.125
    x = x.reshape(m, d)
    tm_norm = min(512, m)
    tm_res = min(512, m)
    kv_shape = (1, b, t, 2, n_heads, head_dim)

    def paged_t(cache):
        c = cache[0].reshape(cache.shape[1], page, -1)
        return jnp.swapaxes(c, 1, 2)

    def kv_from_t(kv_t):
        return jnp.transpose(kv_t.reshape(1, b, 2, n_heads, head_dim, t), (0, 1, 5, 2, 3, 4))

    sh_m, sc_m, g_m, sh_f, sc_f, g_f = mods[0]
    h = norm_mod(x, norm_mix_g[0], sc_m, sh_m, rows_per_mod, tm=tm_norm)
    if sample:
        kv_sb = matmul(h, sb_w_qkv, 0, d, 2 * d).reshape(kv_shape)
        q = matmul(h, sb_w_qkv, 0, 0, d, scale=q_scale)
        o = sb_attention_sample(q, paged_t(cache_kv_sb), page_table, n_heads, head_dim)
    else:
        kv_t = matmul_t(h, sb_w_qkv[0, :, d:].T, b, t)
        kv_sb = kv_from_t(kv_t)
        q = matmul(h, sb_w_qkv, 0, 0, d, out_dtype=BF16, scale=q_scale)
        o = sb_attention_prompt(q, kv_t, tq)
    x = matmul_resid(o, sb_w_o, 0, x, g_m, rows_per_mod, tm=tm_res)
    h = norm_mod(x, norm_ffn_g[0], sc_f, sh_f, rows_per_mod, tm=tm_norm)
    a = matmul_swiglu(h, ffn_w_gate, ffn_w_up, 0, tm=512, tn=1408)
    x = matmul_resid(a, ffn_w_down, 0, x, g_f, rows_per_mod, tm=tm_res)

    sh_m, sc_m, g_m, sh_f, sc_f, g_f = mods[1]
    h = norm_mod(x, norm_mix_g[1], sc_m, sh_m, rows_per_mod, tm=tm_norm)
    n_idx = idx_heads * idx_dim + LANES
    w_idx = jnp.pad(dsa_w_in[:, :, 3 * d:], ((0, 0), (0, 0), (0, n_idx - (dsa_w_in.shape[2] - 3 * d))))
    q_raw = matmul(h, dsa_w_in, 0, 0, d)
    idx = matmul(h, w_idx, 0, 0, n_idx, tn=n_idx)
    qn = headnorm_q(q_raw, dsa_q_g[0], head_dim, q_scale)
    nq = idx_heads * idx_dim
    if sample:
        kv_new = headnorm_kv(matmul(h, dsa_w_in, 0, d, 2 * d), dsa_k_g[0], head_dim)
        kv_dsa = kv_new.reshape(kv_shape)
        k_idx = idx[:, nq:nq + idx_dim].reshape(1, b, t, idx_dim)
        n_past = page_table.shape[1] * page
        topk = min(TOPK_MAX, (n_past + t) // 4)
        scores = dsa_sample_scores(idx[:, :nq], idx[:, nq + idx_dim:nq + idx_dim + idx_heads],
                                   paged_t(cache_kidx_dsa), page_table, idx_heads, idx_dim)
        mask_bias, new_bias = dsa_sample_select(scores, idx, idx_heads, idx_dim, topk)
        o = dsa_attention_sample(qn, mask_bias, new_bias, kv_new, rel_bias,
                                 paged_t(cache_kv_dsa), page_table, n_heads, head_dim)
    else:
        gain = jnp.concatenate([jnp.tile(dsa_k_g[0], n_heads), jnp.ones((d,), F32)]).reshape(2 * d, 1)
        kv_t = matmul_t(h, dsa_w_in[0, :, d:3 * d].T, b, t, norm_rows=d, gain=gain, head_dim=head_dim)
        kv_dsa = kv_from_t(kv_t)
        k_idx_t = matmul_t(h, dsa_w_in[0, :, 3 * d + nq:3 * d + nq + idx_dim].T, b, t, head_dim=idx_dim)
        k_idx = jnp.swapaxes(k_idx_t, 1, 2).reshape(1, b, t, idx_dim)
        topk = min(TOPK_MAX, t // 4)
        mask_bias = dsa_select_prompt(idx, k_idx_t, tq, idx_heads, idx_dim, topk)
        o = dsa_attention_prompt(qn, kv_t, mask_bias, bias_tiles(rel_bias, tq), tq, n_heads)
    x = matmul_resid(o, dsa_w_o, 0, x, g_m, rows_per_mod, tm=tm_res)
    h, combine = norm_mod(x, norm_ffn_g[1], sc_f, sh_f, rows_per_mod, w_router=moe_w_router[0], tm=tm_norm)
    for e in range(n_experts):
        a = matmul_swiglu(h, moe_w_gate[0], moe_w_up[0], e, tm=512, tn=896)
        x = matmul_resid(a, moe_w_down[0], e, x, g_f, rows_per_mod, row_scale=combine, col=e, tm=tm_res)

    return x.reshape(b, t, d), kv_sb, kv_dsa, k_idx


def _forward(x_prompt, x_sample, c_prompt, c_sample, page_table, cache_kv_sb, cache_kv_dsa, cache_kidx_dsa,
             norm_mix_g, norm_ffn_g, w_ada, b_ada, *weights, tq):
    bp, t, d = x_prompt.shape
    bs = x_sample.shape[0]
    depth = w_ada.shape[0]
    mod = ada_modulation(jnp.concatenate([c_prompt, c_sample], axis=0), w_ada, b_ada)
    mods_p = [[mod[i, :bp, k * d:(k + 1) * d].reshape(bp, 1, d) for k in range(6)] for i in range(depth)]
    mods_s = [[mod[i, bp:, k * d:(k + 1) * d].reshape(1, bs, d) for k in range(6)] for i in range(depth)]
    shared = (page_table, cache_kv_sb, cache_kv_dsa, cache_kidx_dsa, norm_mix_g, norm_ffn_g) + weights
    y_s, sb_s, dsa_s, kidx_s = _trunk(x_sample, mods_s, bs, True, *shared, tq=tq)
    y_p, sb_p, dsa_p, kidx_p = _trunk(x_prompt, mods_p, t, False, *shared, tq=tq)
    return (y_p, y_s, sb_p, dsa_p, kidx_p, sb_s, dsa_s, kidx_s)


def kernel(x_prompt, x_sample, c_prompt, c_sample, page_table, cache_kv_sb, cache_kv_dsa, cache_kidx_dsa, norm_mix_g, norm_ffn_g, w_ada, b_ada, sb_w_qkv, sb_w_o, dsa_w_in, dsa_w_o, dsa_q_g, dsa_k_g, rel_bias, ffn_w_gate, ffn_w_up, ffn_w_down, moe_w_router, moe_w_gate, moe_w_up, moe_w_down):
    return _forward(x_prompt, x_sample, c_prompt, c_sample, page_table, cache_kv_sb, cache_kv_dsa,
                    cache_kidx_dsa, norm_mix_g, norm_ffn_g, w_ada, b_ada, sb_w_qkv, sb_w_o, dsa_w_in, dsa_w_o,
                    dsa_q_g, dsa_k_g, rel_bias, ffn_w_gate, ffn_w_up, ffn_w_down, moe_w_router, moe_w_gate,
                    moe_w_up, moe_w_down, tq=256)
```

```python
import functools
import math

import jax
import jax.numpy as jnp
import numpy as np
from jax import lax
from jax.experimental import pallas as pl
from jax.experimental.pallas import tpu as pltpu

F32 = jnp.float32
BF16 = jnp.bfloat16
I32 = jnp.int32
HIGHEST = lax.Precision.HIGHEST

EPS = 1e-6
TOPK_MAX = 256
MAX_DISTANCE = 128
LANES = 128
MASKED = -1e30
DEAD_LOG_WEIGHT = -110.0
INT_MIN = -(2 ** 31)
VMEM_LIMIT = 56 * 1024 * 1024


def _cparams(n_axes, vmem=VMEM_LIMIT):
    return pltpu.CompilerParams(dimension_semantics=("arbitrary",) * n_axes, vmem_limit_bytes=vmem)


def _fit_tile(tile, n, col0=0):
    tile = min(tile, n)
    while n % tile or col0 % tile:
        tile -= LANES
    return tile


def _dot_t(a, b):
    return lax.dot_general(a, b, (((1,), (1,)), ((), ())), preferred_element_type=F32)


def _softplus(s):
    return jnp.maximum(s, 0.0) + jnp.log(1.0 + jnp.exp(-jnp.abs(s)))


def _ada_kernel(c_ref, w_ref, b_ref, o_ref):
    c = c_ref[...]
    a = c * jax.nn.sigmoid(c)
    o_ref[0] = jnp.dot(a.astype(BF16), w_ref[0].astype(BF16), preferred_element_type=F32) + b_ref[0]


def ada_modulation(c, w_ada, b_ada):
    depth, d, n = w_ada.shape
    tn = d
    r = c.shape[0]
    return pl.pallas_call(
        _ada_kernel,
        grid=(depth, n // tn),
        in_specs=[pl.BlockSpec((r, d), lambda l, j: (0, 0)),
                  pl.BlockSpec((1, d, tn), lambda l, j: (l, 0, j)),
                  pl.BlockSpec((1, 1, tn), lambda l, j: (l, 0, j))],
        out_specs=pl.BlockSpec((1, r, tn), lambda l, j: (l, 0, j)),
        out_shape=jax.ShapeDtypeStruct((depth, r, n), F32),
        compiler_params=_cparams(2),
        name="ada_modulation",
    )(c, w_ada, b_ada.reshape(depth, 1, n))


def _norm_mod_kernel(x_ref, g_ref, sc_ref, sh_ref, *rest, router):
    x = x_ref[...]
    y = x * lax.rsqrt(jnp.mean(x * x, axis=-1, keepdims=True) + EPS) * g_ref[...]
    h = y * (1.0 + sc_ref[0]) + sh_ref[0]
    if not router:
        (h_ref,) = rest
        h_ref[...] = h.astype(h_ref.dtype)
        return
    wr_ref, h_ref, comb_ref = rest
    h_ref[...] = h.astype(h_ref.dtype)
    logits = jnp.dot(h.astype(BF16), wr_ref[...].astype(BF16), preferred_element_type=F32)
    n_e = logits.shape[1]
    lane = lax.broadcasted_iota(I32, logits.shape, 1)
    m1 = jnp.max(logits, axis=1, keepdims=True)
    i1 = jnp.min(jnp.where(logits == m1, lane, n_e), axis=1, keepdims=True)
    rest_logits = jnp.where(lane == i1, -jnp.inf, logits)
    m2 = jnp.max(rest_logits, axis=1, keepdims=True)
    i2 = jnp.min(jnp.where(rest_logits == m2, lane, n_e), axis=1, keepdims=True)
    e2 = jnp.exp(m2 - m1)
    g1 = 1.0 / (1.0 + e2)
    comb_ref[...] = jnp.where(lane == i1, g1, 0.0) + jnp.where(lane == i2, e2 * g1, 0.0)


def norm_mod(x, g, sc, sh, rows_per_mod, w_router=None, tm=512):
    m, d = x.shape
    tm = min(tm, m)
    r = sc.shape[1]
    mod_spec = pl.BlockSpec((1, r, d), lambda i: ((i * tm) // rows_per_mod, 0, 0))
    in_specs = [pl.BlockSpec((tm, d), lambda i: (i, 0)), pl.BlockSpec((1, d), lambda i: (0, 0)), mod_spec, mod_spec]
    args = [x, g.reshape(1, d), sc, sh]
    out_specs = pl.BlockSpec((tm, d), lambda i: (i, 0))
    out_shape = jax.ShapeDtypeStruct((m, d), BF16)
    router = w_router is not None
    if router:
        n_e = w_router.shape[1]
        in_specs.append(pl.BlockSpec((d, n_e), lambda i: (0, 0)))
        args.append(w_router)
        out_specs = (out_specs, pl.BlockSpec((tm, n_e), lambda i: (i, 0)))
        out_shape = (out_shape, jax.ShapeDtypeStruct((m, n_e), F32))
    return pl.pallas_call(
        functools.partial(_norm_mod_kernel, router=router),
        grid=(m // tm,), in_specs=in_specs, out_specs=out_specs, out_shape=out_shape,
        compiler_params=_cparams(1), name="norm_mod_router" if router else "norm_mod",
    )(*args)


def _mm_kernel(x_ref, w_ref, o_ref, wbf_ref, *, scale):
    @pl.when(pl.program_id(1) == 0)
    def _():
        wbf_ref[...] = w_ref[0].astype(BF16)
    acc = jnp.dot(x_ref[...].astype(BF16), wbf_ref[...], preferred_element_type=F32)
    if scale != 1.0:
        acc = acc * scale
    o_ref[...] = acc.astype(o_ref.dtype)


def matmul(x, w3, e, col0, n_cols, *, out_dtype=F32, scale=1.0, tm=1024, tn=1024):
    m, k = x.shape
    tm, tn = min(tm, m), _fit_tile(tn, n_cols, col0)
    assert m % tm == 0
    off = col0 // tn
    return pl.pallas_call(
        functools.partial(_mm_kernel, scale=scale),
        grid=(n_cols // tn, m // tm),
        in_specs=[pl.BlockSpec((tm, k), lambda j, i: (i, 0)),
                  pl.BlockSpec((1, k, tn), lambda j, i: (e, 0, j + off))],
        out_specs=pl.BlockSpec((tm, tn), lambda j, i: (i, j)),
        out_shape=jax.ShapeDtypeStruct((m, n_cols), out_dtype),
        scratch_shapes=[pltpu.VMEM((k, tn), BF16)],
        compiler_params=_cparams(2), name="matmul",
    )(x, w3)


def _mm_swiglu_kernel(x_ref, wg_ref, wu_ref, o_ref, wgbf_ref, wubf_ref):
    @pl.when(pl.program_id(1) == 0)
    def _():
        wgbf_ref[...] = wg_ref[0].astype(BF16)
        wubf_ref[...] = wu_ref[0].astype(BF16)
    x = x_ref[...]
    a = jnp.dot(x, wgbf_ref[...], preferred_element_type=F32)
    b = jnp.dot(x, wubf_ref[...], preferred_element_type=F32)
    o_ref[...] = (a * jax.nn.sigmoid(a) * b).astype(o_ref.dtype)


def matmul_swiglu(x, wg3, wu3, e, *, tm=1024, tn=512):
    m, k = x.shape
    n = wg3.shape[2]
    tm, tn = min(tm, m), _fit_tile(tn, n)
    assert m % tm == 0
    wspec = pl.BlockSpec((1, k, tn), lambda j, i: (e, 0, j))
    return pl.pallas_call(
        _mm_swiglu_kernel,
        grid=(n // tn, m // tm),
        in_specs=[pl.BlockSpec((tm, k), lambda j, i: (i, 0)), wspec, wspec],
        out_specs=pl.BlockSpec((tm, tn), lambda j, i: (i, j)),
        out_shape=jax.ShapeDtypeStruct((m, n), BF16),
        scratch_shapes=[pltpu.VMEM((k, tn), BF16), pltpu.VMEM((k, tn), BF16)],
        compiler_params=_cparams(2), name="matmul_swiglu",
    )(x, wg3, wu3)


def _mm_resid_kernel(x_ref, w_ref, res_ref, gate_ref, *rest, col):
    if col is None:
        o_ref, wbf_ref = rest
    else:
        rs_ref, o_ref, wbf_ref = rest

    @pl.when(pl.program_id(1) == 0)
    def _():
        wbf_ref[...] = w_ref[0].astype(BF16)
    y = jnp.dot(x_ref[...], wbf_ref[...], preferred_element_type=F32)
    if col is not None:
        y = y * rs_ref[:, col:col + 1]
    o_ref[...] = res_ref[...] + gate_ref[0] * y


def matmul_resid(x, w3, e, res, gate, rows_per_gate, row_scale=None, col=None, *, tm=512, tn=512):
    m, k = x.shape
    n = w3.shape[2]
    tm, tn = min(tm, m), _fit_tile(tn, n)
    assert m % tm == 0
    r = gate.shape[1]
    in_specs = [pl.BlockSpec((tm, k), lambda j, i: (i, 0)),
                pl.BlockSpec((1, k, tn), lambda j, i: (e, 0, j)),
                pl.BlockSpec((tm, tn), lambda j, i: (i, j)),
                pl.BlockSpec((1, r, tn), lambda j, i: ((i * tm) // rows_per_gate, 0, j))]
    args = [x, w3, res, gate]
    if col is not None:
        in_specs.append(pl.BlockSpec((tm, row_scale.shape[1]), lambda j, i: (i, 0)))
        args.append(row_scale)
    return pl.pallas_call(
        functools.partial(_mm_resid_kernel, col=col),
        grid=(n // tn, m // tm),
        in_specs=in_specs,
        out_specs=pl.BlockSpec((tm, tn), lambda j, i: (i, j)),
        out_shape=jax.ShapeDtypeStruct((m, n), F32),
        scratch_shapes=[pltpu.VMEM((k, tn), BF16)],
        compiler_params=_cparams(2), name="matmul_resid",
    )(*args)


def _mm_t_kernel(x_ref, wt_ref, g_ref, o_ref, wbf_ref, *, norm_tiles, head_dim):
    @pl.when(jnp.logical_and(pl.program_id(1) == 0, pl.program_id(2) == 0))
    def _():
        wbf_ref[...] = wt_ref[...].astype(BF16)
    acc = _dot_t(wbf_ref[...], x_ref[0])
    tn, tm = acc.shape

    @pl.when(pl.program_id(0) < norm_tiles)
    def _():
        a3 = acc.reshape(tn // head_dim, head_dim, tm)
        inv = lax.rsqrt(jnp.mean(a3 * a3, axis=1, keepdims=True) + EPS)
        o_ref[0] = (a3 * inv * g_ref[...].reshape(tn // head_dim, head_dim, 1)).reshape(tn, tm)

    @pl.when(pl.program_id(0) >= norm_tiles)
    def _():
        o_ref[0] = acc


def matmul_t(x, wt, b, t, *, norm_rows=0, gain=None, head_dim=64, tm=1024, tn=512):
    m, k = x.shape
    n = wt.shape[0]
    tm, tn = min(tm, t), math.gcd(min(tn, n), norm_rows) if norm_rows else min(tn, n)
    assert t % tm == 0 and n % tn == 0 and norm_rows % tn == 0 and tn % head_dim == 0
    if gain is None:
        gain = jnp.ones((n, 1), F32)
    return pl.pallas_call(
        functools.partial(_mm_t_kernel, norm_tiles=norm_rows // tn, head_dim=head_dim),
        grid=(n // tn, b, t // tm),
        in_specs=[pl.BlockSpec((1, tm, k), lambda j, bi, i: (bi, i, 0)),
                  pl.BlockSpec((tn, k), lambda j, bi, i: (j, 0)),
                  pl.BlockSpec((tn, 1), lambda j, bi, i: (j, 0))],
        out_specs=pl.BlockSpec((1, tn, tm), lambda j, bi, i: (bi, j, i)),
        out_shape=jax.ShapeDtypeStruct((b, n, t), F32),
        scratch_shapes=[pltpu.VMEM((tn, k), BF16)],
        compiler_params=_cparams(3), name="matmul_t",
    )(x.reshape(b, t, k), wt, gain)


def _head_scale(x, seg_ref, exp_ref, head_dim):
    ss = jnp.dot(x * x, seg_ref[...], preferred_element_type=F32, precision=HIGHEST)
    inv = lax.rsqrt(ss * (1.0 / head_dim) + EPS)
    return jnp.dot(inv, exp_ref[...], preferred_element_type=F32, precision=HIGHEST)


def _headnorm_q_kernel(x_ref, g_ref, seg_ref, exp_ref, o_ref, *, head_dim, scale):
    x = x_ref[...]
    y = x * _head_scale(x, seg_ref, exp_ref, head_dim) * g_ref[...]
    o_ref[...] = (y * scale).astype(o_ref.dtype)


def _headnorm_kv_kernel(x_ref, g_ref, seg_ref, exp_ref, o_ref, *, head_dim):
    d = g_ref.shape[1]
    k = x_ref[:, :d]
    o_ref[:, :d] = k * _head_scale(k, seg_ref, exp_ref, head_dim) * g_ref[...]
    o_ref[:, d:] = x_ref[:, d:]


def _head_mats(d, head_dim):
    head_of = np.arange(d) // head_dim
    seg = (head_of[:, None] == np.arange(d // head_dim)[None, :]).astype(np.float32)
    return jnp.asarray(seg), jnp.asarray(seg.T)


def headnorm_q(x, g, head_dim, scale, tm=512):
    m, d = x.shape
    tm = min(tm, m)
    seg, expand = _head_mats(d, head_dim)
    nh = d // head_dim
    return pl.pallas_call(
        functools.partial(_headnorm_q_kernel, head_dim=head_dim, scale=scale),
        grid=(m // tm,),
        in_specs=[pl.BlockSpec((tm, d), lambda i: (i, 0)), pl.BlockSpec((1, d), lambda i: (0, 0)),
                  pl.BlockSpec((d, nh), lambda i: (0, 0)), pl.BlockSpec((nh, d), lambda i: (0, 0))],
        out_specs=pl.BlockSpec((tm, d), lambda i: (i, 0)),
        out_shape=jax.ShapeDtypeStruct((m, d), BF16),
        compiler_params=_cparams(1), name="headnorm_q",
    )(x, jnp.tile(g, nh).reshape(1, d), seg, expand)


def headnorm_kv(kv, g, head_dim, tm=512):
    m, d2 = kv.shape
    d = d2 // 2
    tm = min(tm, m)
    seg, expand = _head_mats(d, head_dim)
    nh = d // head_dim
    return pl.pallas_call(
        functools.partial(_headnorm_kv_kernel, head_dim=head_dim),
        grid=(m // tm,),
        in_specs=[pl.BlockSpec((tm, d2), lambda i: (i, 0)), pl.BlockSpec((1, d), lambda i: (0, 0)),
                  pl.BlockSpec((d, nh), lambda i: (0, 0)), pl.BlockSpec((nh, d), lambda i: (0, 0))],
        out_specs=pl.BlockSpec((tm, d2), lambda i: (i, 0)),
        out_shape=jax.ShapeDtypeStruct((m, d2), F32),
        compiler_params=_cparams(1), name="headnorm_kv",
    )(kv, jnp.tile(g, nh).reshape(1, d), seg, expand)


def _suffix_matrix(n):
    r = lax.broadcasted_iota(I32, (n, n), 0)
    c = lax.broadcasted_iota(I32, (n, n), 1)
    return jnp.where(r > c, 1.0, 0.0).astype(BF16)


def _suffix_sum(l, u):
    hi = l.astype(BF16)
    lo = (l - hi.astype(F32)).astype(BF16)
    return jnp.dot(hi, u, preferred_element_type=F32) + jnp.dot(lo, u, preferred_element_type=F32)


def _sb_prompt_kernel(q_ref, k_ref, v_ref, o_ref, *, tq):
    tk = tq
    qi = pl.program_id(2)
    q2 = q_ref[0]
    lane = lax.broadcasted_iota(I32, (tq, LANES), 1)
    first = lane < LANES // 2
    qh = (jnp.where(first, q2, 0).astype(BF16), jnp.where(first, 0, q2).astype(BF16))
    u = _suffix_matrix(tk)
    row = lax.broadcasted_iota(I32, (tq, tk), 0)
    col = lax.broadcasted_iota(I32, (tq, tk), 1)
    strictly_before = col < row

    def block(j, carries, diagonal):
        start = pl.multiple_of(j * tk, tk)
        k2 = k_ref[0, :, pl.ds(start, tk)].astype(BF16)
        v2 = v_ref[0, :, pl.ds(start, tk)].astype(BF16)
        pvs, new_carries = [], []
        for hh in range(2):
            s = jnp.dot(qh[hh], k2, preferred_element_type=F32)
            sp = _softplus(s)
            l = -sp
            if diagonal:
                l = jnp.where(strictly_before, l, 0.0)
            after = _suffix_sum(l, u)
            logw = (s - sp) + after + carries[hh]
            w = jnp.exp(logw)
            if diagonal:
                w = jnp.where(strictly_before, w, 0.0)
            pvs.append(_dot_t(w.astype(BF16), v2))
            new_carries.append(carries[hh] + after[:, :1] + l[:, :1])
        return jnp.where(first, pvs[0], pvs[1]), new_carries

    zero = jnp.zeros((tq, 1), F32)
    acc, carries = block(qi, [zero, zero], True)

    def alive_of(cs):
        return (jnp.max(jnp.maximum(cs[0], cs[1])) > DEAD_LOG_WEIGHT).astype(I32)

    def cond(st):
        return jnp.logical_and(st[0] >= 0, st[1] > 0)

    def body(st):
        j, _, acc, c0, c1 = st
        pv, cs = block(j, [c0, c1], False)
        return j - 1, alive_of(cs), acc + pv, cs[0], cs[1]

    st = lax.while_loop(cond, body, (qi - 1, alive_of(carries), acc, carries[0], carries[1]))
    o_ref[0] = st[2].astype(o_ref.dtype)


def sb_attention_prompt(q, kv_t, tq):
    b, d2, t = kv_t.shape
    d = d2 // 2
    ng = d // LANES
    out = pl.pallas_call(
        functools.partial(_sb_prompt_kernel, tq=tq),
        grid=(b, ng, t // tq),
        in_specs=[pl.BlockSpec((1, tq, LANES), lambda bi, g, i: (bi, i, g)),
                  pl.BlockSpec((1, LANES, t), lambda bi, g, i: (bi, g, 0)),
                  pl.BlockSpec((1, LANES, t), lambda bi, g, i: (bi, ng + g, 0))],
        out_specs=pl.BlockSpec((1, tq, LANES), lambda bi, g, i: (bi, i, g)),
        out_shape=jax.ShapeDtypeStruct((b, t, d), BF16),
        compiler_params=_cparams(3), name="sb_attention_prompt",
    )(q.reshape(b, t, d), kv_t, kv_t)
    return out.reshape(b * t, d)


def _block_diag_q(q_row, n_heads, head_dim):
    d = q_row.shape[1]
    head_of_lane = lax.broadcasted_iota(I32, (n_heads, d), 1) // head_dim
    mine = head_of_lane == lax.broadcasted_iota(I32, (n_heads, d), 0)
    return jnp.where(mine, jnp.broadcast_to(q_row, (n_heads, d)), 0.0), mine


def _sb_sample_kernel(pt_ref, q_ref, cache_ref, o_ref, buf_ref, sem_ref, *, n_heads, head_dim):
    b = pl.program_id(0)
    n_pages = pt_ref.shape[1]
    page = buf_ref.shape[2]
    d = n_heads * head_dim

    def page_copy(p, slot):
        return pltpu.make_async_copy(cache_ref.at[pt_ref[b, p]], buf_ref.at[slot], sem_ref.at[slot])

    page_copy(n_pages - 1, 0).start()
    qbd, mine = _block_diag_q(q_ref[0], n_heads, head_dim)
    qbd = qbd.astype(BF16)
    u = _suffix_matrix(page)

    def cond(st):
        return jnp.logical_and(st[0] >= 0, st[1] > 0)

    def body(st):
        p, _, carry, acc = st
        slot = (n_pages - 1 - p) % 2
        page_copy(p, slot).wait()

        @pl.when(p >= 1)
        def _():
            page_copy(p - 1, 1 - slot).start()

        kp = buf_ref[slot, :d, :].astype(BF16)
        vp = buf_ref[slot, d:, :].astype(BF16)
        s = jnp.dot(qbd, kp, preferred_element_type=F32)
        sp = _softplus(s)
        l = -sp
        after = _suffix_sum(l, u)
        w = jnp.exp((s - sp) + after + carry)
        acc = acc + _dot_t(w.astype(BF16), vp)
        carry = carry + after[:, :1] + l[:, :1]
        alive = (jnp.max(carry) > DEAD_LOG_WEIGHT).astype(I32)
        return p - 1, alive, carry, acc

    st = lax.while_loop(cond, body, (jnp.int32(n_pages - 1), jnp.int32(1),
                                     jnp.zeros((n_heads, 1), F32), jnp.zeros((n_heads, d), F32)))
    p_end = st[0]

    @pl.when(p_end >= 0)
    def _():
        page_copy(p_end, (n_pages - 1 - p_end) % 2).wait()

    o_ref[0] = jnp.sum(jnp.where(mine, st[3], 0.0), axis=0, keepdims=True).astype(o_ref.dtype)


def sb_attention_sample(q, cache, page_table, n_heads, head_dim):
    b, d = q.shape
    page = cache.shape[2]
    out = pl.pallas_call(
        functools.partial(_sb_sample_kernel, n_heads=n_heads, head_dim=head_dim),
        grid_spec=pltpu.PrefetchScalarGridSpec(
            num_scalar_prefetch=1, grid=(b,),
            in_specs=[pl.BlockSpec((1, 1, d), lambda i, pt: (i, 0, 0)),
                      pl.BlockSpec(memory_space=pl.ANY)],
            out_specs=pl.BlockSpec((1, 1, d), lambda i, pt: (i, 0, 0)),
            scratch_shapes=[pltpu.VMEM((2, 2 * d, page), F32), pltpu.SemaphoreType.DMA((2,))]),
        out_shape=jax.ShapeDtypeStruct((b, 1, d), BF16),
        compiler_params=_cparams(1), name="sb_attention_sample",
    )(page_table, q.reshape(b, 1, d), cache)
    return out.reshape(b, d)


def _ordered_key(score):
    bits = lax.bitcast_convert_type(score, I32)
    key = jnp.where(bits < 0, bits ^ 0x7FFFFFFF, bits)
    return jnp.where(score == 0.0, 0, key)


def _head_weights(w_idx, idx_heads, idx_dim):
    return (w_idx * idx_heads ** -0.5).astype(BF16).astype(F32) * idx_dim ** -0.5


def _relu_bf16(s):
    return jnp.maximum(s.astype(BF16), 0.0).astype(F32)


def _bisect_threshold(count_ge, topk, shape):
    c0 = count_ge(jnp.zeros(shape, I32))
    lo = jnp.where(c0 >= topk, 0, INT_MIN).astype(I32)
    cnt = jnp.where(c0 >= topk, c0, topk + 1)

    def cond(st):
        return jnp.logical_and(st[0] < 31, jnp.max(jnp.abs(st[2] - topk)) > 0)

    def step(st):
        it, lo, cnt = st
        cand = lo | lax.shift_left(jnp.int32(1), 30 - it)
        c = count_ge(cand)
        return it + 1, jnp.where(c >= topk, cand, lo), jnp.where(c >= topk, c, cnt)

    return lax.while_loop(cond, step, (jnp.int32(0), lo, cnt))[1]


def _tie_cutoff(count_tie_before, need, n_bits, shape):
    def step(it, j):
        cand = j | lax.shift_left(jnp.int32(1), n_bits - 1 - it)
        return jnp.where(count_tie_before(cand) < need, cand, j)

    return lax.fori_loop(0, n_bits, step, jnp.zeros(shape, I32))


def _selected(key, thr, col, cutoff):
    return jnp.where(key > thr, 1, jnp.where(key == thr, jnp.where(col <= cutoff, 1, 0), 0))


def _dsa_select_kernel(qi_ref, qw_ref, kt_ref, o_ref, key_ref, *, tq, idx_heads, idx_dim, topk):
    tk = tq
    i = pl.program_id(1)
    n_blocks = o_ref.shape[2] // tk
    qi = qi_ref[0].astype(BF16)
    wi = _head_weights(qw_ref[0][:, idx_dim:idx_dim + idx_heads], idx_heads, idx_dim)
    row = lax.broadcasted_iota(I32, (tq, tk), 0)
    col = lax.broadcasted_iota(I32, (tq, tk), 1)
    zeros = jnp.zeros((idx_dim, tk), BF16)

    def score_block(j, _):
        start = pl.multiple_of(j * tk, tk)
        kt = kt_ref[0, :, pl.ds(start, tk)].astype(BF16)
        ka = jnp.concatenate([kt, zeros], axis=0)
        kb = jnp.concatenate([zeros, kt], axis=0)
        score = jnp.zeros((tq, tk), F32)
        for g in range(idx_heads // 2):
            q2 = qi[:, g * LANES:(g + 1) * LANES]
            score = score + wi[:, 2 * g:2 * g + 1] * _relu_bf16(jnp.dot(q2, ka, preferred_element_type=F32))
            score = score + wi[:, 2 * g + 1:2 * g + 2] * _relu_bf16(jnp.dot(q2, kb, preferred_element_type=F32))
        score = jnp.where(col + (j - i) * tk <= row, score, -jnp.inf)
        key_ref[:, pl.ds(start, tk)] = _ordered_key(score)
        return 0

    lax.fori_loop(0, i + 1, score_block, 0)

    def count_ge(cand):
        def body(j, acc):
            hit = jnp.where(key_ref[:, pl.ds(pl.multiple_of(j * tk, tk), tk)] >= cand, 1, 0)
            for c in range(tk // LANES):
                acc = acc + hit[:, c * LANES:(c + 1) * LANES]
            return acc
        acc = lax.fori_loop(0, i + 1, body, jnp.zeros((tq, LANES), I32))
        return jnp.sum(acc, axis=1, keepdims=True)

    thr = _bisect_threshold(count_ge, topk, (tq, 1))
    need = topk - count_ge(thr + 1)
    t_keys = n_blocks * tk

    def count_tie_before(cand):
        def body(j, acc):
            kt = key_ref[:, pl.ds(pl.multiple_of(j * tk, tk), tk)]
            hit = jnp.where(kt == thr, jnp.where(col + j * tk < cand, 1, 0), 0)
            for c in range(tk // LANES):
                acc = acc + hit[:, c * LANES:(c + 1) * LANES]
            return acc
        acc = lax.fori_loop(0, i + 1, body, jnp.zeros((tq, LANES), I32))
        return jnp.sum(acc, axis=1, keepdims=True)

    cutoff = lax.cond(jnp.max(count_ge(thr)) > topk,
                      lambda: _tie_cutoff(count_tie_before, need, (t_keys - 1).bit_length(), (tq, 1)),
                      lambda: jnp.full((tq, 1), t_keys, I32))

    def write_block(j, _):
        start = pl.multiple_of(j * tk, tk)
        keep = _selected(key_ref[:, pl.ds(start, tk)], thr, col + j * tk, cutoff)
        keep = jnp.where(col + (j - i) * tk <= row, keep, 0)
        o_ref[0, :, pl.ds(start, tk)] = jnp.where(keep > 0, 0.0, MASKED).astype(o_ref.dtype)
        return 0

    lax.fori_loop(0, i + 1, write_block, 0)

    def fill_block(j, _):
        o_ref[0, :, pl.ds(pl.multiple_of(j * tk, tk), tk)] = jnp.full((tq, tk), MASKED, o_ref.dtype)
        return 0

    lax.fori_loop(i + 1, n_blocks, fill_block, 0)


def dsa_select_prompt(idx, k_idx_t, tq, idx_heads, idx_dim, topk):
    b, _, t = k_idx_t.shape
    w = idx.shape[1]
    nq = idx_heads * idx_dim
    assert nq % LANES == 0 and w == nq + LANES and idx_dim * 2 == LANES
    idx3 = idx.reshape(b, t, w)
    kw_block = nq // LANES
    return pl.pallas_call(
        functools.partial(_dsa_select_kernel, tq=tq, idx_heads=idx_heads, idx_dim=idx_dim, topk=topk),
        grid=(b, t // tq),
        in_specs=[pl.BlockSpec((1, tq, nq), lambda bi, i: (bi, i, 0)),
                  pl.BlockSpec((1, tq, LANES), lambda bi, i: (bi, i, kw_block)),
                  pl.BlockSpec((1, idx_dim, t), lambda bi, i: (bi, 0, 0))],
        out_specs=pl.BlockSpec((1, tq, t), lambda bi, i: (bi, i, 0)),
        out_shape=jax.ShapeDtypeStruct((b, t, t), BF16),
        scratch_shapes=[pltpu.VMEM((tq, t), I32)],
        compiler_params=_cparams(2), name="dsa_select_prompt",
    )(idx3, idx3, k_idx_t)


def _t5_bucket(rel, n_buckets):
    n = jnp.maximum(rel, 0)
    max_exact = n_buckets // 2
    nf = jnp.maximum(n, 1).astype(F32)
    large = max_exact + (jnp.log(nf / max_exact) / math.log(MAX_DISTANCE / max_exact)
                         * (n_buckets - max_exact)).astype(I32)
    large = jnp.minimum(large, n_buckets - 1)
    return jnp.where(n < max_exact, n, large)


def _bias_tiles_kernel(relb_ref, o_ref, *, tq, n_buckets):
    h = pl.program_id(0)
    row = lax.broadcasted_iota(I32, (tq, tq), 0)
    col = lax.broadcasted_iota(I32, (tq, tq), 1)
    far = relb_ref[n_buckets - 1, h]
    for s in range(2):
        bucket = _t5_bucket(row - col + s * tq, n_buckets)
        val = jnp.zeros((tq, tq), F32)
        for bk in range(n_buckets):
            val = jnp.where(bucket == bk, relb_ref[bk, h] - far, val)
        o_ref[0, s] = val


def bias_tiles(rel_bias, tq):
    n_buckets, n_heads = rel_bias.shape
    assert tq >= MAX_DISTANCE
    return pl.pallas_call(
        functools.partial(_bias_tiles_kernel, tq=tq, n_buckets=n_buckets),
        grid=(n_heads,),
        in_specs=[pl.BlockSpec(memory_space=pltpu.SMEM)],
        out_specs=pl.BlockSpec((1, 2, tq, tq), lambda h: (h, 0, 0, 0)),
        out_shape=jax.ShapeDtypeStruct((n_heads, 2, tq, tq), F32),
        compiler_params=_cparams(1), name="bias_tiles",
    )(rel_bias)


def _dsa_attn_kernel(qt_ref, kt_ref, q_ref, kv_ref, mb_ref, bias_ref, o_ref, m_ref, l_ref, acc_ref,
                     *, tq, n_heads):
    p = pl.program_id(1)
    qi, kj = qt_ref[p], kt_ref[p]
    d = q_ref.shape[2]
    lane = lax.broadcasted_iota(I32, (tq, LANES), 1)
    first = lane < LANES // 2
    first_rows = lax.broadcasted_iota(I32, (LANES, tq), 0) < LANES // 2

    @pl.when(kj == 0)
    def _():
        m_ref[...] = jnp.full(m_ref.shape, MASKED, F32)
        l_ref[...] = jnp.zeros(l_ref.shape, F32)
        acc_ref[...] = jnp.zeros(acc_ref.shape, F32)

    def run(with_bias):
        mb = mb_ref[0].astype(F32)
        for g in range(n_heads // 2):
            cols = slice(g * LANES, (g + 1) * LANES)
            q2 = q_ref[0, :, cols]
            k2 = kv_ref[0, g * LANES:(g + 1) * LANES, :].astype(BF16)
            v2 = kv_ref[0, d + g * LANES:d + (g + 1) * LANES, :].astype(BF16)
            v_aug = (jnp.where(first_rows, v2, 1.0).astype(BF16), jnp.where(first_rows, 1.0, v2).astype(BF16))
            pvs, alphas = [], []
            for hh in range(2):
                h = 2 * g + hh
                qh = jnp.where(first, q2, 0) if hh == 0 else jnp.where(first, 0, q2)
                s = jnp.dot(qh.astype(BF16), k2, preferred_element_type=F32) + mb
                if with_bias:
                    s = s + bias_ref[h, qi - kj]
                m_old = m_ref[h]
                m_new = jnp.maximum(m_old, jnp.max(s, axis=1, keepdims=True))
                alphas.append(jnp.exp(m_old - m_new))
                m_ref[h] = m_new
                pr = jnp.exp(s - jnp.concatenate([m_new] * (s.shape[1] // LANES), axis=1))
                pvs.append(_dot_t(pr.astype(BF16), v_aug[hh]))
            acc_ref[:, cols] = (jnp.where(first, alphas[0], alphas[1]) * acc_ref[:, cols]
                                + jnp.where(first, pvs[0], pvs[1]))
            l_ref[:, cols] = (jnp.where(first, alphas[1], alphas[0]) * l_ref[:, cols]
                              + jnp.where(first, pvs[1], pvs[0]))

    near = kj >= qi - 1

    @pl.when(near)
    def _():
        run(True)

    @pl.when(jnp.logical_not(near))
    def _():
        run(False)

    @pl.when(kj == qi)
    def _():
        for g in range(n_heads // 2):
            cols = slice(g * LANES, (g + 1) * LANES)
            inv = 1.0 / pltpu.roll(l_ref[:, cols], LANES // 2, axis=1)
            o_ref[0, :, cols] = (acc_ref[:, cols] * inv).astype(o_ref.dtype)


def dsa_attention_prompt(q, kv_t, mask_bias, bias, tq, n_heads):
    b, _, t = kv_t.shape
    d = q.shape[1]
    nb = t // tq
    pairs = [(i, j) for i in range(nb) for j in range(i + 1)]
    q_tab = jnp.asarray(np.array([p[0] for p in pairs], np.int32))
    k_tab = jnp.asarray(np.array([p[1] for p in pairs], np.int32))
    out = pl.pallas_call(
        functools.partial(_dsa_attn_kernel, tq=tq, n_heads=n_heads),
        grid_spec=pltpu.PrefetchScalarGridSpec(
            num_scalar_prefetch=2, grid=(b, len(pairs)),
            in_specs=[pl.BlockSpec((1, tq, d), lambda bi, p, qt, kt: (bi, qt[p], 0)),
                      pl.BlockSpec((1, 2 * d, tq), lambda bi, p, qt, kt: (bi, 0, kt[p])),
                      pl.BlockSpec((1, tq, tq), lambda bi, p, qt, kt: (bi, qt[p], kt[p])),
                      pl.BlockSpec((n_heads, 2, tq, tq), lambda bi, p, qt, kt: (0, 0, 0, 0))],
            out_specs=pl.BlockSpec((1, tq, d), lambda bi, p, qt, kt: (bi, qt[p], 0)),
            scratch_shapes=[pltpu.VMEM((n_heads, tq, LANES), F32), pltpu.VMEM((tq, d), F32),
                            pltpu.VMEM((tq, d), F32)]),
        out_shape=jax.ShapeDtypeStruct((b, t, d), BF16),
        compiler_params=_cparams(2), name="dsa_attention_prompt",
    )(q_tab, k_tab, q.reshape(b, t, d), kv_t, mask_bias, bias)
    return out.reshape(b * t, d)


PAGES_PER_STEP = 8


def _page_specs(page_shape, n_pages_per_step):
    def spec(u):
        return pl.BlockSpec((1,) + page_shape, lambda bi, s, pt: (pt[bi, s * n_pages_per_step + u], 0, 0))
    return [spec(u) for u in range(n_pages_per_step)]


def _dsa_sample_score_kernel(pt_ref, qi_ref, wi_ref, *rest, idx_heads, idx_dim):
    page_refs, o_ref = rest[:-1], rest[-1]
    qi = qi_ref[0].astype(BF16)
    w = _head_weights(wi_ref[0], idx_heads, idx_dim)
    page = page_refs[0].shape[2]
    for u_, ref in enumerate(page_refs):
        s = jnp.dot(qi, ref[0].astype(BF16), preferred_element_type=F32)
        o_ref[0, :, u_ * page:(u_ + 1) * page] = jnp.sum(w * _relu_bf16(s), axis=0, keepdims=True)


def dsa_sample_scores(qi, wi, cache_kidx, page_table, idx_heads, idx_dim):
    b, n_pages = page_table.shape
    page = cache_kidx.shape[2]
    u = math.gcd(PAGES_PER_STEP, n_pages)
    out = pl.pallas_call(
        functools.partial(_dsa_sample_score_kernel, idx_heads=idx_heads, idx_dim=idx_dim),
        grid_spec=pltpu.PrefetchScalarGridSpec(
            num_scalar_prefetch=1, grid=(b, n_pages // u),
            in_specs=[pl.BlockSpec((1, idx_heads, idx_dim), lambda bi, s, pt: (bi, 0, 0)),
                      pl.BlockSpec((1, idx_heads, 1), lambda bi, s, pt: (bi, 0, 0))]
                     + _page_specs((idx_dim, page), u),
            out_specs=pl.BlockSpec((1, 1, u * page), lambda bi, s, pt: (bi, 0, s))),
        out_shape=jax.ShapeDtypeStruct((b, 1, n_pages * page), F32),
        compiler_params=_cparams(2), name="dsa_sample_scores",
    )(page_table, qi.reshape(b, idx_heads, idx_dim), wi.reshape(b, idx_heads, 1), *([cache_kidx] * u))
    return out.reshape(b, n_pages * page)


def _dsa_sample_select_kernel(score_ref, idx_ref, rep_ref, seg_ref, mb_ref, newb_ref, key_ref,
                              *, idx_heads, idx_dim, topk):
    b, n_past = score_ref.shape
    nq = idx_heads * idx_dim
    qi = idx_ref[:, :nq].astype(BF16).astype(F32)
    ki = idx_ref[:, nq:nq + idx_dim].astype(BF16).astype(F32)
    wi = _head_weights(idx_ref[:, nq + idx_dim:nq + idx_dim + idx_heads], idx_heads, idx_dim)
    prod = qi * jnp.dot(ki, rep_ref[...], preferred_element_type=F32, precision=HIGHEST)
    s_new = jnp.dot(prod, seg_ref[...], preferred_element_type=F32, precision=HIGHEST)
    score_new = jnp.sum(wi * _relu_bf16(s_new), axis=1, keepdims=True)
    key_new = _ordered_key(score_new)
    key_ref[...] = _ordered_key(score_ref[...])
    chunk = 1024

    def count_ge(cand):
        def body(c, acc):
            hit = jnp.where(key_ref[:, pl.ds(pl.multiple_of(c * chunk, chunk), chunk)] >= cand, 1, 0)
            for cc in range(chunk // LANES):
                acc = acc + hit[:, cc * LANES:(cc + 1) * LANES]
            return acc
        acc = lax.fori_loop(0, n_past // chunk, body, jnp.zeros((b, LANES), I32))
        return jnp.sum(acc, axis=1, keepdims=True) + jnp.where(key_new >= cand, 1, 0)

    thr = _bisect_threshold(count_ge, topk, (b, 1))
    need = topk - count_ge(thr + 1)
    col = lax.broadcasted_iota(I32, (b, chunk), 1)

    def count_tie_before(cand):
        def body(c, acc):
            kt = key_ref[:, pl.ds(pl.multiple_of(c * chunk, chunk), chunk)]
            hit = jnp.where(kt == thr, jnp.where(col + c * chunk < cand, 1, 0), 0)
            for cc in range(chunk // LANES):
                acc = acc + hit[:, cc * LANES:(cc + 1) * LANES]
            return acc
        acc = lax.fori_loop(0, n_past // chunk, body, jnp.zeros((b, LANES), I32))
        return (jnp.sum(acc, axis=1, keepdims=True)
                + jnp.where(key_new == thr, jnp.where(n_past < cand, 1, 0), 0))

    cutoff = _tie_cutoff(count_tie_before, need, n_past.bit_length(), (b, 1))

    def write_chunk(c, _):
        start = pl.multiple_of(c * chunk, chunk)
        keep = _selected(key_ref[:, pl.ds(start, chunk)], thr, col + c * chunk, cutoff)
        mb_ref[:, pl.ds(start, chunk)] = jnp.where(keep > 0, 0.0, MASKED)
        return 0

    lax.fori_loop(0, n_past // chunk, write_chunk, 0)
    new_bias = jnp.where(_selected(key_new, thr, n_past, cutoff) > 0, 0.0, MASKED)
    newb_ref[...] = jnp.broadcast_to(new_bias, newb_ref.shape)


def dsa_sample_select(scores, idx, idx_heads, idx_dim, topk):
    b, n_past = scores.shape
    assert n_past % 1024 == 0
    nq = idx_heads * idx_dim
    seg = jnp.asarray((np.arange(nq)[:, None] // idx_dim == np.arange(idx_heads)[None, :]).astype(np.float32))
    rep = jnp.asarray((np.arange(idx_dim)[:, None] == np.arange(nq)[None, :] % idx_dim).astype(np.float32))
    return pl.pallas_call(
        functools.partial(_dsa_sample_select_kernel, idx_heads=idx_heads, idx_dim=idx_dim, topk=topk),
        out_shape=(jax.ShapeDtypeStruct((b, n_past), F32), jax.ShapeDtypeStruct((b, LANES), F32)),
        scratch_shapes=[pltpu.VMEM((b, n_past), I32)],
        compiler_params=pltpu.CompilerParams(vmem_limit_bytes=VMEM_LIMIT), name="dsa_sample_select",
    )(scores, idx, rep, seg)


def _dsa_sample_attn_kernel(pt_ref, q_ref, mb_ref, newb_ref, kvn_ref, relbt_ref, *rest,
                            n_heads, head_dim, n_buckets, n_pages):
    page_refs, (o_ref, m_ref, l_ref, acc_ref) = rest[:-4], rest[-4:]
    step = pl.program_id(1)
    u = len(page_refs)
    n_steps = n_pages // u
    page = page_refs[0].shape[2]
    d = n_heads * head_dim
    n_past = n_pages * page
    qbd, mine = _block_diag_q(q_ref[0].astype(F32), n_heads, head_dim)
    qbd = qbd.astype(BF16)
    relbt = relbt_ref[...]

    @pl.when(step == 0)
    def _():
        m_ref[...] = jnp.full(m_ref.shape, MASKED, F32)
        l_ref[...] = jnp.zeros(l_ref.shape, F32)
        acc_ref[...] = jnp.zeros(acc_ref.shape, F32)

    def update(s, values_fn):
        m_old = m_ref[...]
        m_new = jnp.maximum(m_old, jnp.max(s, axis=1, keepdims=True))
        alpha = jnp.exp(m_old - m_new)
        pr = jnp.exp(s - m_new)
        l_ref[...] = alpha * l_ref[...] + jnp.sum(pr, axis=1, keepdims=True)
        m_ref[...] = m_new
        acc_ref[...] = alpha * acc_ref[...] + values_fn(pr)

    bucket_iota = lax.broadcasted_iota(I32, (n_buckets, page), 0)
    for u_, ref in enumerate(page_refs):
        k_pos = (step * u + u_) * page + lax.broadcasted_iota(I32, (1, page), 1)
        onehot = jnp.where(bucket_iota == _t5_bucket(n_past - k_pos, n_buckets), 1.0, 0.0)
        bias = jnp.dot(relbt, onehot, preferred_element_type=F32, precision=HIGHEST)
        kp = ref[0, :d, :].astype(BF16)
        vp = ref[0, d:, :].astype(BF16)
        s = (jnp.dot(qbd, kp, preferred_element_type=F32) + bias
             + mb_ref[0, :, u_ * page:(u_ + 1) * page])
        update(s, lambda pr, vp=vp: _dot_t(pr.astype(BF16), vp))

    @pl.when(step == n_steps - 1)
    def _():
        k_new = kvn_ref[0, :, :d].astype(BF16).astype(F32)
        v_new = kvn_ref[0, :, d:]
        s_new = (jnp.sum(qbd.astype(F32) * k_new, axis=1, keepdims=True)
                 + relbt[:, 0:1] + newb_ref[0, :, 0:1])
        update(s_new, lambda pr: pr * v_new)
        out = acc_ref[...] * (1.0 / l_ref[...])
        o_ref[0] = jnp.sum(jnp.where(mine, out, 0.0), axis=0, keepdims=True).astype(o_ref.dtype)


def dsa_attention_sample(q, mask_bias, new_bias, kv_new, rel_bias, cache, page_table, n_heads, head_dim):
    b, d = q.shape
    n_pages = page_table.shape[1]
    page = cache.shape[2]
    u = math.gcd(PAGES_PER_STEP, n_pages)
    n_buckets = rel_bias.shape[0]
    out = pl.pallas_call(
        functools.partial(_dsa_sample_attn_kernel, n_heads=n_heads, head_dim=head_dim, n_buckets=n_buckets,
                          n_pages=n_pages),
        grid_spec=pltpu.PrefetchScalarGridSpec(
            num_scalar_prefetch=1, grid=(b, n_pages // u),
            in_specs=[pl.BlockSpec((1, 1, d), lambda bi, s, pt: (bi, 0, 0)),
                      pl.BlockSpec((1, 1, u * page), lambda bi, s, pt: (bi, 0, s)),
                      pl.BlockSpec((1, 1, LANES), lambda bi, s, pt: (bi, 0, 0)),
                      pl.BlockSpec((1, 1, 2 * d), lambda bi, s, pt: (bi, 0, 0)),
                      pl.BlockSpec((n_heads, n_buckets), lambda bi, s, pt: (0, 0))]
                     + _page_specs((2 * d, page), u),
            out_specs=pl.BlockSpec((1, 1, d), lambda bi, s, pt: (bi, 0, 0)),
            scratch_shapes=[pltpu.VMEM((n_heads, 1), F32), pltpu.VMEM((n_heads, 1), F32),
                            pltpu.VMEM((n_heads, d), F32)]),
        out_shape=jax.ShapeDtypeStruct((b, 1, d), BF16),
        compiler_params=_cparams(2), name="dsa_attention_sample",
    )(page_table, q.reshape(b, 1, d), mask_bias.reshape(b, 1, -1), new_bias.reshape(b, 1, LANES),
      kv_new.reshape(b, 1, 2 * d), rel_bias.T, *([cache] * u))
    return out.reshape(b, d)


def _trunk(x, mods, rows_per_mod, sample, page_table, cache_kv_sb, cache_kv_dsa, cache_kidx_dsa,
           norm_mix_g, norm_ffn_g, sb_w_qkv, sb_w_o, dsa_w_in, dsa_w_o, dsa_q_g, dsa_k_g, rel_bias,
           ffn_w_gate, ffn_w_up, ffn_w_down, moe_w_router, moe_w_gate, moe_w_up, moe_w_down, tq):
    b, t, d = x.shape
    m = b * t
    head_dim = dsa_q_g.shape[1]
    n_heads = d // head_dim
    idx_dim = cache_kidx_dsa.shape[-1]
    idx_heads = (dsa_w_in.shape[2] - 3 * d - idx_dim) // (idx_dim + 1)
    page = cache_kv_sb.shape[2]
    n_experts = moe_w_router.shape[2]
    q_scale = head_dim ** -0.5
    assert head_dim * 2 == LANES and q_scale == 0.125
    x = x.reshape(m, d)
    tm_norm = min(512, m)
    tm_res = min(512, m)
    kv_shape = (1, b, t, 2, n_heads, head_dim)

    def paged_t(cache):
        c = cache[0].reshape(cache.shape[1], page, -1)
        return jnp.swapaxes(c, 1, 2)

    def kv_from_t(kv_t):
        return jnp.transpose(kv_t.reshape(1, b, 2, n_heads, head_dim, t), (0, 1, 5, 2, 3, 4))

    sh_m, sc_m, g_m, sh_f, sc_f, g_f = mods[0]
    h = norm_mod(x, norm_mix_g[0], sc_m, sh_m, rows_per_mod, tm=tm_norm)
    if sample:
        kv_sb = matmul(h, sb_w_qkv, 0, d, 2 * d).reshape(kv_shape)
        q = matmul(h, sb_w_qkv, 0, 0, d, scale=q_scale)
        o = sb_attention_sample(q, paged_t(cache_kv_sb), page_table, n_heads, head_dim)
    else:
        kv_t = matmul_t(h, sb_w_qkv[0, :, d:].T, b, t)
        kv_sb = kv_from_t(kv_t)
        q = matmul(h, sb_w_qkv, 0, 0, d, out_dtype=BF16, scale=q_scale)
        o = sb_attention_prompt(q, kv_t, tq)
    x = matmul_resid(o, sb_w_o, 0, x, g_m, rows_per_mod, tm=tm_res)
    h = norm_mod(x, norm_ffn_g[0], sc_f, sh_f, rows_per_mod, tm=tm_norm)
    a = matmul_swiglu(h, ffn_w_gate, ffn_w_up, 0, tm=512, tn=1408)
    x = matmul_resid(a, ffn_w_down, 0, x, g_f, rows_per_mod, tm=tm_res)

    sh_m, sc_m, g_m, sh_f, sc_f, g_f = mods[1]
    h = norm_mod(x, norm_mix_g[1], sc_m, sh_m, rows_per_mod, tm=tm_norm)
    n_idx = idx_heads * idx_dim + LANES
    w_idx = jnp.pad(dsa_w_in[:, :, 3 * d:], ((0, 0), (0, 0), (0, n_idx - (dsa_w_in.shape[2] - 3 * d))))
    q_raw = matmul(h, dsa_w_in, 0, 0, d)
    idx = matmul(h, w_idx, 0, 0, n_idx, tn=n_idx)
    qn = headnorm_q(q_raw, dsa_q_g[0], head_dim, q_scale)
    nq = idx_heads * idx_dim
    if sample:
        kv_new = headnorm_kv(matmul(h, dsa_w_in, 0, d, 2 * d), dsa_k_g[0], head_dim)
        kv_dsa = kv_new.reshape(kv_shape)
        k_idx = idx[:, nq:nq + idx_dim].reshape(1, b, t, idx_dim)
        n_past = page_table.shape[1] * page
        topk = min(TOPK_MAX, (n_past + t) // 4)
        scores = dsa_sample_scores(idx[:, :nq], idx[:, nq + idx_dim:nq + idx_dim + idx_heads],
                                   paged_t(cache_kidx_dsa), page_table, idx_heads, idx_dim)
        mask_bias, new_bias = dsa_sample_select(scores, idx, idx_heads, idx_dim, topk)
        o = dsa_attention_sample(qn, mask_bias, new_bias, kv_new, rel_bias,
                                 paged_t(cache_kv_dsa), page_table, n_heads, head_dim)
    else:
        gain = jnp.concatenate([jnp.tile(dsa_k_g[0], n_heads), jnp.ones((d,), F32)]).reshape(2 * d, 1)
        kv_t = matmul_t(h, dsa_w_in[0, :, d:3 * d].T, b, t, norm_rows=d, gain=gain, head_dim=head_dim)
        kv_dsa = kv_from_t(kv_t)
        k_idx_t = matmul_t(h, dsa_w_in[0, :, 3 * d + nq:3 * d + nq + idx_dim].T, b, t, head_dim=idx_dim)
        k_idx = jnp.swapaxes(k_idx_t, 1, 2).reshape(1, b, t, idx_dim)
        topk = min(TOPK_MAX, t // 4)
        mask_bias = dsa_select_prompt(idx, k_idx_t, tq, idx_heads, idx_dim, topk)
        o = dsa_attention_prompt(qn, kv_t, mask_bias, bias_tiles(rel_bias, tq), tq, n_heads)
    x = matmul_resid(o, dsa_w_o, 0, x, g_m, rows_per_mod, tm=tm_res)
    h, combine = norm_mod(x, norm_ffn_g[1], sc_f, sh_f, rows_per_mod, w_router=moe_w_router[0], tm=tm_norm)
    for e in range(n_experts):
        a = matmul_swiglu(h, moe_w_gate[0], moe_w_up[0], e, tm=512, tn=896)
        x = matmul_resid(a, moe_w_down[0], e, x, g_f, rows_per_mod, row_scale=combine, col=e, tm=tm_res)

    return x.reshape(b, t, d), kv_sb, kv_dsa, k_idx


def _forward(x_prompt, x_sample, c_prompt, c_sample, page_table, cache_kv_sb, cache_kv_dsa, cache_kidx_dsa,
             norm_mix_g, norm_ffn_g, w_ada, b_ada, *weights, tq):
    bp, t, d = x_prompt.shape
    bs = x_sample.shape[0]
    depth = w_ada.shape[0]
    mod = ada_modulation(jnp.concatenate([c_prompt, c_sample], axis=0), w_ada, b_ada)
    mods_p = [[mod[i, :bp, k * d:(k + 1) * d].reshape(bp, 1, d) for k in range(6)] for i in range(depth)]
    mods_s = [[mod[i, bp:, k * d:(k + 1) * d].reshape(1, bs, d) for k in range(6)] for i in range(depth)]
    shared = (page_table, cache_kv_sb, cache_kv_dsa, cache_kidx_dsa, norm_mix_g, norm_ffn_g) + weights
    y_s, sb_s, dsa_s, kidx_s = _trunk(x_sample, mods_s, bs, True, *shared, tq=tq)
    y_p, sb_p, dsa_p, kidx_p = _trunk(x_prompt, mods_p, t, False, *shared, tq=tq)
    return (y_p, y_s, sb_p, dsa_p, kidx_p, sb_s, dsa_s, kidx_s)


def kernel(x_prompt, x_sample, c_prompt, c_sample, page_table, cache_kv_sb, cache_kv_dsa, cache_kidx_dsa, norm_mix_g, norm_ffn_g, w_ada, b_ada, sb_w_qkv, sb_w_o, dsa_w_in, dsa_w_o, dsa_q_g, dsa_k_g, rel_bias, ffn_w_gate, ffn_w_up, ffn_w_down, moe_w_router, moe_w_gate, moe_w_up, moe_w_down):
    return _forward(x_prompt, x_sample, c_prompt, c_sample, page_table, cache_kv_sb, cache_kv_dsa,
                    cache_kidx_dsa, norm_mix_g, norm_ffn_g, w_ada, b_ada, sb_w_qkv, sb_w_o, dsa_w_in, dsa_w_o,
                    dsa_q_g, dsa_k_g, rel_bias, ffn_w_gate, ffn_w_up, ffn_w_down, moe_w_router, moe_w_gate,
                    moe_w_up, moe_w_down, tq=256)
```
